```python
import math
import jax, jax.numpy as jnp
from jax import lax
import numpy as np

D_MODEL = 2048
BATCH = 4
SEQ = 8192
DEPTH = 2
DEC_BATCH = 16
DEC_SEQ = 64
PAST_LEN = 2048

CHUNK = 64
EPS = 1e-6
SSD_HEADS = 32
SSD_HEAD_DIM = 64
D_INNER = SSD_HEADS * SSD_HEAD_DIM
SSD_GROUPS = 4
D_STATE = 128
CONV_W = 4
XBC_DIM = D_INNER + 2 * SSD_GROUPS * D_STATE
GM_CHUNK = 128
GM_GROUPS = 8
GM_GROUP_DIM = 128
GM_WIDTH = GM_GROUPS * GM_GROUP_DIM
D_FF = 5632
PLE_DIM = 256
IN_COLS = D_INNER + XBC_DIM + SSD_HEADS + 2 * GM_WIDTH + 2 * D_MODEL

kernel_name = 'hybrid_ssd_gmlp_streaming_step'


def _rmsnorm(x, g):
    xf = x.astype(jnp.float32)
    y = xf * lax.rsqrt(jnp.mean(xf * xf, axis=-1, keepdims=True) + EPS)
    return (y * g.astype(jnp.float32)).astype(x.dtype)


def _swiglu(h, w_gate, w_up, w_down):
    return (jax.nn.silu(h @ w_gate) * (h @ w_up)) @ w_down


def _causal_dwconv(x, buf, w, b):
    L = x.shape[1]
    xp = jnp.concatenate([buf.astype(x.dtype), x], axis=1)
    y = b
    for k in range(CONV_W):
        y = y + xp[:, k:k + L] * w[k]
    return y, xp[:, xp.shape[1] - (CONV_W - 1):]


def _ssd_scan(x, dt, a, bm, cm, h0):
    bsz, L, H, P = x.shape
    G, N = bm.shape[2], bm.shape[3]
    hg = H // G
    Q = min(CHUNK, L)
    nc = L // Q

    def blocks(t):
        return jnp.moveaxis(t.reshape((bsz, nc, Q) + t.shape[2:]), 1, 0)

    mask = jnp.tril(jnp.ones((Q, Q), dtype=bool))

    def step(h, inp):
        xc, dtc, bc, cc = inp
        cum = jnp.cumsum(dtc * a, axis=1)
        seg = cum[:, :, None, :] - cum[:, None, :, :]
        decay = jnp.exp(jnp.where(mask[None, :, :, None], seg, -jnp.inf))
        cb = jnp.repeat(jnp.einsum('btgn,bsgn->btsg', cc, bc), hg, axis=-1)
        xdt = xc * dtc[..., None]
        y_diag = jnp.einsum('btsh,bshp->bthp', cb * decay, xdt)
        h_grp = h.reshape(bsz, G, hg, P, N)
        y_off = jnp.einsum('btgn,bgkpn->btgkp', cc, h_grp).reshape(bsz, Q, H, P) * jnp.exp(cum)[..., None]
        dec_end = jnp.exp(cum[:, -1:, :] - cum)
        bh = jnp.repeat(bc, hg, axis=2)
        h_new = h * jnp.exp(cum[:, -1])[:, :, None, None] + jnp.einsum('bsh,bshp,bshn->bhpn', dec_end, xdt, bh)
        return h_new, y_diag + y_off

    h_last, ys = lax.scan(step, h0, (blocks(x), blocks(dt), blocks(bm), blocks(cm)))
    y = jnp.moveaxis(ys, 0, 1).reshape(bsz, L, H, P)
    return y, h_last


def _ssd_branch(z, xbc_raw, dt_raw, conv_buf, h0, conv_w, conv_b, dt_bias, a_log, d_skip, norm_g):
    xbc, new_buf = _causal_dwconv(xbc_raw, conv_buf, conv_w, conv_b)
    xbc = jax.nn.silu(xbc).astype(jnp.float32)
    bsz, L, _ = xbc.shape
    gn = SSD_GROUPS * D_STATE
    xs = xbc[..., :D_INNER].reshape(bsz, L, SSD_HEADS, SSD_HEAD_DIM)
    bm = xbc[..., D_INNER:D_INNER + gn].reshape(bsz, L, SSD_GROUPS, D_STATE)
    cm = xbc[..., D_INNER + gn:].reshape(bsz, L, SSD_GROUPS, D_STATE)
    dt = jax.nn.softplus(dt_raw.astype(jnp.float32) + dt_bias.astype(jnp.float32))
    a = -jnp.exp(a_log.astype(jnp.float32))
    y, h_last = _ssd_scan(xs, dt, a, bm, cm, h0.astype(jnp.float32))
    y = y + xs * d_skip.astype(jnp.float32)[:, None]
    yg = (y.reshape(bsz, L, D_INNER) * jax.nn.silu(z.astype(jnp.float32))).reshape(bsz, L, SSD_GROUPS, D_INNER // SSD_GROUPS)
    yg = yg * lax.rsqrt(jnp.mean(yg * yg, axis=-1, keepdims=True) + EPS)
    y = yg.reshape(bsz, L, D_INNER) * norm_g.astype(jnp.float32)
    return y.astype(z.dtype), new_buf, h_last


def _gmlp_branch(uv, ln_g, ln_b, ws, bs):
    uv = jax.nn.gelu(uv)
    u, v = uv[..., :GM_WIDTH], uv[..., GM_WIDTH:]
    vf = v.astype(jnp.float32)
    mu = jnp.mean(vf, axis=-1, keepdims=True)
    var = jnp.mean(jnp.square(vf - mu), axis=-1, keepdims=True)
    vn = ((vf - mu) * lax.rsqrt(var + EPS) * ln_g.astype(jnp.float32) + ln_b.astype(jnp.float32)).astype(v.dtype)
    bsz, L, _ = u.shape
    T = min(GM_CHUNK, L)
    nc = L // T
    mask = jnp.tril(jnp.ones((T, T), dtype=bool))
    w = jnp.where(mask[None], ws[:, :T, :T], 0.0)
    vc = vn.reshape(bsz, nc, T, GM_GROUPS, GM_GROUP_DIM)
    bias = jnp.transpose(bs[:, :T])
    mixed = jnp.einsum('gts,bnsgc->bntgc', w, vc) + bias[None, None, :, :, None]
    return u * mixed.reshape(bsz, L, GM_WIDTH), vn


def _trunk(x, pe, conv_state, ssm_state, w, keep_v):
    new_conv, new_ssm, new_v = [], [], []
    o1 = D_INNER
    o2 = o1 + XBC_DIM
    o3 = o2 + SSD_HEADS
    o4 = o3 + 2 * GM_WIDTH
    for i in range(DEPTH):
        h = _rmsnorm(x, w['ffn1_norm'][i])
        x = x + 0.5 * _swiglu(h, w['ffn1_w_gate'][i], w['ffn1_w_up'][i], w['ffn1_w_down'][i])
        h = _rmsnorm(x, w['mix_norm'][i])
        proj = h @ w['w_in'][i]
        z, xbc, dt_raw = proj[..., :o1], proj[..., o1:o2], proj[..., o2:o3]
        uv, gates = proj[..., o3:o4], proj[..., o4:]
        ya, buf, hs = _ssd_branch(z, xbc, dt_raw, conv_state[i], ssm_state[i], w['conv_w'][i], w['conv_b'][i],
                                  w['dt_bias'][i], w['a_log'][i], w['d_skip'][i], w['ssd_norm'][i])
        yb, vn = _gmlp_branch(uv, w['gm_ln_g'][i], w['gm_ln_b'][i], w['gm_ws'][i], w['gm_bs'][i])
        ga = jax.nn.sigmoid(gates[..., :D_MODEL])
        gb = jax.nn.sigmoid(gates[..., D_MODEL:])
        merged = ga * (ya @ w['w_branch_ssd'][i]) + gb * (yb @ w['w_branch_gmlp'][i])
        x = x + merged @ w['w_out'][i]
        h = _rmsnorm(x, w['ffn2_norm'][i])
        x = x + 0.5 * _swiglu(h, w['ffn2_w_gate'][i], w['ffn2_w_up'][i], w['ffn2_w_down'][i])
        h = _rmsnorm(x, w['ple_norm'][i])
        x = x + (pe[i] @ w['ple_w_proj'][i]) * jax.nn.sigmoid(h @ w['ple_w_gate'][i])
        new_conv.append(buf)
        new_ssm.append(hs)
        if keep_v:
            new_v.append(vn)
    y = _rmsnorm(x, w['final_norm'])
    v_out = jnp.stack(new_v) if keep_v else None
    return y, jnp.stack(new_conv), jnp.stack(new_ssm), v_out


def setup_inputs(seed: int = 0) -> dict:
    key = jax.random.key(seed)
    ks = list(jax.random.split(key, 48))
    cnt = [0]

    def nk():
        k = ks[cnt[0]]
        cnt[0] += 1
        return k

    def nrm(shape, scale):
        return scale * jax.random.normal(nk(), shape, jnp.float32)

    def gain(shape):
        return 1.0 + 0.05 * jax.random.normal(nk(), shape, jnp.float32)

    u = jax.random.uniform(nk(), (DEPTH, SSD_HEADS), jnp.float32)
    dt0 = jnp.exp(u * (math.log(0.1) - math.log(1e-3)) + math.log(1e-3))
    dt_bias = dt0 + jnp.log(-jnp.expm1(-dt0))
    a_log = jnp.log(jax.random.uniform(nk(), (DEPTH, SSD_HEADS), jnp.float32, minval=1.0, maxval=16.0))
    return {
        'x_prompt': nrm((BATCH, SEQ, D_MODEL), 1.0),
        'x_sample': nrm((DEC_BATCH, DEC_SEQ, D_MODEL), 1.0),
        'p_prompt': nrm((DEPTH, BATCH, SEQ, PLE_DIM), 1.0),
        'p_sample': nrm((DEPTH, DEC_BATCH, DEC_SEQ, PLE_DIM), 1.0),
        'state_conv': nrm((DEPTH, DEC_BATCH, CONV_W - 1, XBC_DIM), 1.0),
        'state_ssm': nrm((DEPTH, DEC_BATCH, SSD_HEADS, SSD_HEAD_DIM, D_STATE), 0.1),
        'ffn1_norm': gain((DEPTH, D_MODEL)),
        'ffn1_w_gate': nrm((DEPTH, D_MODEL, D_FF), D_MODEL ** -0.5),
        'ffn1_w_up': nrm((DEPTH, D_MODEL, D_FF), D_MODEL ** -0.5),
        'ffn1_w_down': nrm((DEPTH, D_FF, D_MODEL), D_FF ** -0.5),
        'mix_norm': gain((DEPTH, D_MODEL)),
        'w_in': nrm((DEPTH, D_MODEL, IN_COLS), D_MODEL ** -0.5),
        'conv_w': nrm((DEPTH, CONV_W, XBC_DIM), CONV_W ** -0.5),
        'conv_b': nrm((DEPTH, XBC_DIM), 0.01),
        'dt_bias': dt_bias,
        'a_log': a_log,
        'd_skip': gain((DEPTH, SSD_HEADS)),
        'ssd_norm': gain((DEPTH, D_INNER)),
        'gm_ln_g': gain((DEPTH, GM_WIDTH)),
        'gm_ln_b': nrm((DEPTH, GM_WIDTH), 0.01),
        'gm_ws': nrm((DEPTH, GM_GROUPS, GM_CHUNK, GM_CHUNK), GM_CHUNK ** -0.5),
        'gm_bs': gain((DEPTH, GM_GROUPS, GM_CHUNK)),
        'w_branch_ssd': nrm((DEPTH, D_INNER, D_MODEL), D_INNER ** -0.5),
        'w_branch_gmlp': nrm((DEPTH, GM_WIDTH, D_MODEL), GM_WIDTH ** -0.5),
        'w_out': nrm((DEPTH, D_MODEL, D_MODEL), D_MODEL ** -0.5),
        'ffn2_norm': gain((DEPTH, D_MODEL)),
        'ffn2_w_gate': nrm((DEPTH, D_MODEL, D_FF), D_MODEL ** -0.5),
        'ffn2_w_up': nrm((DEPTH, D_MODEL, D_FF), D_MODEL ** -0.5),
        'ffn2_w_down': nrm((DEPTH, D_FF, D_MODEL), D_FF ** -0.5),
        'ple_norm': gain((DEPTH, D_MODEL)),
        'ple_w_gate': nrm((DEPTH, D_MODEL, D_MODEL), D_MODEL ** -0.5),
        'ple_w_proj': nrm((DEPTH, PLE_DIM, D_MODEL), PLE_DIM ** -0.5),
        'final_norm': gain((D_MODEL,)),
    }


def reference(x_prompt, x_sample, p_prompt, p_sample, state_conv, state_ssm,
              ffn1_norm, ffn1_w_gate, ffn1_w_up, ffn1_w_down,
              mix_norm, w_in, conv_w, conv_b, dt_bias, a_log, d_skip, ssd_norm,
              gm_ln_g, gm_ln_b, gm_ws, gm_bs,
              w_branch_ssd, w_branch_gmlp, w_out,
              ffn2_norm, ffn2_w_gate, ffn2_w_up, ffn2_w_down,
              ple_norm, ple_w_gate, ple_w_proj, final_norm):
    w = {
        'ffn1_norm': ffn1_norm, 'ffn1_w_gate': ffn1_w_gate, 'ffn1_w_up': ffn1_w_up, 'ffn1_w_down': ffn1_w_down,
        'mix_norm': mix_norm, 'w_in': w_in, 'conv_w': conv_w, 'conv_b': conv_b,
        'dt_bias': dt_bias, 'a_log': a_log, 'd_skip': d_skip, 'ssd_norm': ssd_norm,
        'gm_ln_g': gm_ln_g, 'gm_ln_b': gm_ln_b, 'gm_ws': gm_ws, 'gm_bs': gm_bs,
        'w_branch_ssd': w_branch_ssd, 'w_branch_gmlp': w_branch_gmlp, 'w_out': w_out,
        'ffn2_norm': ffn2_norm, 'ffn2_w_gate': ffn2_w_gate, 'ffn2_w_up': ffn2_w_up, 'ffn2_w_down': ffn2_w_down,
        'ple_norm': ple_norm, 'ple_w_gate': ple_w_gate, 'ple_w_proj': ple_w_proj, 'final_norm': final_norm,
    }
    bp = x_prompt.shape[0]
    conv0 = jnp.zeros((DEPTH, bp, CONV_W - 1, XBC_DIM), x_prompt.dtype)
    ssm0 = jnp.zeros((DEPTH, bp, SSD_HEADS, SSD_HEAD_DIM, D_STATE), jnp.float32)
    y_prompt, conv_prompt, ssm_prompt, _ = _trunk(x_prompt, p_prompt, conv0, ssm0, w, False)
    y_sample, conv_sample, ssm_sample, gmlp_v_sample = _trunk(x_sample, p_sample, state_conv, state_ssm, w, True)
    return (y_prompt, y_sample, ssm_prompt, conv_prompt, ssm_sample, conv_sample, gmlp_v_sample)
```

```python
import functools

import jax
import jax.numpy as jnp
from jax import lax
from jax.experimental import pallas as pl
from jax.experimental.pallas import tpu as pltpu

F32 = jnp.float32
BF16 = jnp.bfloat16

D_MODEL = 2048
SSD_HEADS = 32
SSD_HEAD_DIM = 64
D_INNER = SSD_HEADS * SSD_HEAD_DIM
SSD_GROUPS = 4
HEADS_PER_GROUP = SSD_HEADS // SSD_GROUPS
D_STATE = 128
CONV_W = 4
BC_DIM = SSD_GROUPS * D_STATE
XBC_DIM = D_INNER + 2 * BC_DIM
GM_CHUNK = 128
GM_GROUPS = 8
GM_GROUP_DIM = 128
GM_WIDTH = GM_GROUPS * GM_GROUP_DIM
EPS = 1e-6

LANES = 128
SUBLANES = 8
MIB = 1024 * 1024

PROJ_TN = 1024
XBC_OFF = 0
Z_OFF = 4096
UV_OFF = 6144
GATE_OFF = 8192
PROJ_COLS = 12288
PROJ_PAD_TILE = XBC_DIM // PROJ_TN
PROJ_W_COLS = XBC_DIM + D_INNER + 2 * GM_WIDTH + 2 * D_MODEL

SSD_CHUNK = 128
CONV_TAIL_ROW = SUBLANES - (CONV_W - 1)


def _params(semantics, vmem_mib):
    return pltpu.CompilerParams(dimension_semantics=semantics, vmem_limit_bytes=vmem_mib * MIB)


def _rmsnorm(x, g):
    return x * lax.rsqrt(jnp.mean(x * x, axis=-1, keepdims=True) + EPS) * g


def _dot(a, b):
    return jnp.dot(a, b, preferred_element_type=F32)


def _row_tile(t, want):
    return want if t % want == 0 else t


def _ffn_body(x_ref, g_ref, wg_ref, wu_ref, wd_ref, o_ref, h_ref):
    j = pl.program_id(1)

    @pl.when(j == 0)
    def _():
        h_ref[...] = _rmsnorm(x_ref[...], g_ref[...]).astype(BF16)
        o_ref[...] = jnp.zeros_like(o_ref)

    h = h_ref[...]
    act = (jax.nn.silu(_dot(h, wg_ref[...])) * _dot(h, wu_ref[...])).astype(BF16)
    o_ref[...] += _dot(act, wd_ref[...])

    @pl.when(j == pl.num_programs(1) - 1)
    def _():
        o_ref[...] = x_ref[...] + 0.5 * o_ref[...]


def _ffn(x, g, wg, wu, wd):
    t, d = x.shape
    dff = wg.shape[1]
    tm = _row_tile(t, 512)
    tf = 512
    return pl.pallas_call(
        _ffn_body,
        out_shape=jax.ShapeDtypeStruct((t, d), F32),
        grid=(t // tm, dff // tf),
        in_specs=[
            pl.BlockSpec((tm, d), lambda i, j: (i, 0)),
            pl.BlockSpec((1, d), lambda i, j: (0, 0)),
            pl.BlockSpec((d, tf), lambda i, j: (0, j)),
            pl.BlockSpec((d, tf), lambda i, j: (0, j)),
            pl.BlockSpec((tf, d), lambda i, j: (j, 0)),
        ],
        out_specs=pl.BlockSpec((tm, d), lambda i, j: (i, 0)),
        scratch_shapes=[pltpu.VMEM((tm, d), BF16)],
        compiler_params=_params(("parallel", "arbitrary"), 48),
        name="ffn",
    )(x, g, wg, wu, wd)


def _inproj_body(x_ref, g_ref, w_ref, wdt_ref, o_ref, dt_ref, h_ref):
    @pl.when(pl.program_id(1) == 0)
    def _():
        h = _rmsnorm(x_ref[...], g_ref[...]).astype(BF16)
        h_ref[...] = h
        dt_ref[...] = _dot(h, wdt_ref[...])

    o_ref[...] = _dot(h_ref[...], w_ref[...])


def _inproj(x, g, w, wdt):
    t, d = x.shape
    tm = _row_tile(t, 512)
    n_tiles = PROJ_W_COLS // PROJ_TN
    return pl.pallas_call(
        _inproj_body,
        out_shape=(jax.ShapeDtypeStruct((t, PROJ_COLS), F32), jax.ShapeDtypeStruct((t, LANES), F32)),
        grid=(t // tm, n_tiles),
        in_specs=[
            pl.BlockSpec((tm, d), lambda i, j: (i, 0)),
            pl.BlockSpec((1, d), lambda i, j: (0, 0)),
            pl.BlockSpec((d, PROJ_TN), lambda i, j: (0, j)),
            pl.BlockSpec((d, LANES), lambda i, j: (0, 0)),
        ],
        out_specs=(
            pl.BlockSpec((tm, PROJ_TN), lambda i, j: (i, jnp.where(j >= PROJ_PAD_TILE, j + 1, j))),
            pl.BlockSpec((tm, LANES), lambda i, j: (i, 0)),
        ),
        scratch_shapes=[pltpu.VMEM((tm, d), BF16)],
        compiler_params=_params(("parallel", "arbitrary"), 40),
        name="inproj",
    )(x, g, w, wdt)


def _split_bf16(v, n):
    parts = []
    rest = v
    for _ in range(n):
        p = rest.astype(BF16)
        parts.append(p)
        rest = rest - p.astype(F32)
    return parts


def _dot_split(a_f32, b_bf16, n):
    parts = _split_bf16(a_f32, n)
    out = _dot(parts[0], b_bf16)
    for p in parts[1:]:
        out = out + _dot(p, b_bf16)
    return out


def _transpose_rows_to_lanes(x):
    q = x.shape[0]
    if q == LANES:
        return x.T
    pad = jnp.zeros((LANES - q, x.shape[1]), x.dtype)
    return jnp.concatenate([x, pad], axis=0).T[:, :q]


def _ssd_body(xbc_ref, z_ref, dt_ref, conv0_ref, ssm0_ref, cw_ref, cb_ref, dtb_ref, alog_ref, drep_ref,
              ng_ref, rep_ref, ya_ref, convo_ref, ssmo_ref, xp_ref, st_ref, *, q, nc):
    c = pl.program_id(1)
    pairs_per_group = HEADS_PER_GROUP // 2
    gw = HEADS_PER_GROUP * SSD_HEAD_DIM

    @pl.when(c == 0)
    def _():
        xp_ref[CONV_TAIL_ROW:SUBLANES, :] = conv0_ref[0]
        for k in range(SSD_HEADS // 2):
            g, kk = divmod(k, pairs_per_group)
            st_ref[g, :, kk * LANES:(kk + 1) * LANES] = ssm0_ref[0, k].T

    xp_ref[SUBLANES:SUBLANES + q, :] = xbc_ref[...]
    acc = cb_ref[...]
    for k in range(CONV_W):
        acc = acc + xp_ref[CONV_TAIL_ROW + k:CONV_TAIL_ROW + k + q, :] * cw_ref[k:k + 1, :]
    xc = jax.nn.silu(acc)

    @pl.when(c == nc - 1)
    def _():
        convo_ref[0] = xp_ref[q + CONV_TAIL_ROW:q + SUBLANES, :]

    xp_ref[CONV_TAIL_ROW:SUBLANES, :] = xp_ref[q + CONV_TAIL_ROW:q + SUBLANES, :]

    xs = xc[:, :D_INNER]

    dt = jax.nn.softplus(dt_ref[...] + dtb_ref[...])
    d_a = dt * (-jnp.exp(alog_ref[...]))
    row = lax.broadcasted_iota(jnp.int32, (q, q), 0)
    col = lax.broadcasted_iota(jnp.int32, (q, q), 1)
    tril = row >= col
    cum = _dot_split_lhs01(jnp.where(tril, 1.0, 0.0).astype(BF16), d_a, 3)
    cum_last = cum[q - 1:q, :]
    ecum = jnp.exp(cum)
    dec_end = jnp.exp(cum_last - cum)
    cdec = jnp.broadcast_to(jnp.exp(cum_last), (SUBLANES, LANES))

    stacked = jnp.concatenate([dt, ecum, dec_end, cdec], axis=0)
    expanded = _dot_split(stacked, rep_ref[...], 2)
    dt_e = expanded[:q]
    ecum_e = expanded[q:2 * q]
    dec_e = expanded[2 * q:3 * q]
    cdec_e = expanded[3 * q:3 * q + 1]

    xdt = xs * dt_e
    xdt_b = xdt.astype(BF16)
    xdtd_b = (xdt * dec_e).astype(BF16)
    cum_t = _transpose_rows_to_lanes(cum)
    left = lax.broadcasted_iota(jnp.int32, (q, LANES), 1) < SSD_HEAD_DIM
    zero_b = jnp.zeros((q, LANES), BF16)

    def masked_decay(cb, h):
        seg = cum[:, h:h + 1] - cum_t[h:h + 1, :]
        return (cb * jnp.exp(jnp.where(tril, seg, -jnp.inf))).astype(BF16)

    ys = []
    for g in range(SSD_GROUPS):
        b_f32 = xc[:, D_INNER + g * D_STATE:D_INNER + (g + 1) * D_STATE]
        b_g = b_f32.astype(BF16)
        c_g = xc[:, D_INNER + BC_DIM + g * D_STATE:D_INNER + BC_DIM + (g + 1) * D_STATE].astype(BF16)
        cb = lax.dot_general(c_g, b_g, (((1,), (1,)), ((), ())), preferred_element_type=F32)
        st_g = st_ref[g]
        y_off = _dot(c_g, st_g.astype(BF16)) * ecum_e[:, g * gw:(g + 1) * gw]
        for kk in range(pairs_per_group):
            pair = g * pairs_per_group + kk
            m0 = masked_decay(cb, 2 * pair)
            m1 = masked_decay(cb, 2 * pair + 1)
            xpair = xdt_b[:, pair * LANES:(pair + 1) * LANES]
            x_l = jnp.where(left, xpair, zero_b)
            x_r = jnp.where(left, zero_b, xpair)
            if q % LANES == 0:
                y_diag = _dot(jnp.concatenate([m0, m1], axis=1), jnp.concatenate([x_l, x_r], axis=0))
            else:
                y_diag = _dot(m0, x_l) + _dot(m1, x_r)
            ys.append(y_diag + y_off[:, kk * LANES:(kk + 1) * LANES])
        b_t = _transpose_rows_to_lanes(b_f32).astype(BF16)
        st_ref[g] = st_g * cdec_e[:, g * gw:(g + 1) * gw] + _dot(b_t, xdtd_b[:, g * gw:(g + 1) * gw])

    y = jnp.concatenate(ys, axis=1) + xs * drep_ref[...]

    yg = y * jax.nn.silu(z_ref[...])
    normed = []
    for g in range(SSD_GROUPS):
        blk = yg[:, g * gw:(g + 1) * gw]
        normed.append(blk * lax.rsqrt(jnp.mean(blk * blk, axis=-1, keepdims=True) + EPS))
    ya_ref[...] = (jnp.concatenate(normed, axis=1) * ng_ref[...]).astype(BF16)

    @pl.when(c == nc - 1)
    def _():
        for k in range(SSD_HEADS // 2):
            g, kk = divmod(k, pairs_per_group)
            ssmo_ref[0, k] = st_ref[g, :, kk * LANES:(kk + 1) * LANES].T


def _dot_split_lhs01(lhs01_bf16, rhs_f32, n):
    parts = _split_bf16(rhs_f32, n)
    out = _dot(lhs01_bf16, parts[0])
    for p in parts[1:]:
        out = out + _dot(lhs01_bf16, p)
    return out


def _ssd(proj, dt_raw, conv0, ssm0, lw, nseq, seqlen):
    t = proj.shape[0]
    q = min(SSD_CHUNK, seqlen)
    nc = seqlen // q
    ssm0_pairs = ssm0.reshape(nseq, SSD_HEADS // 2, 2 * SSD_HEAD_DIM, D_STATE)
    const = lambda b, c: (0, 0)
    rowblk = lambda b, c: (b * nc + c, 0)
    ya, conv_out, ssm_out = pl.pallas_call(
        functools.partial(_ssd_body, q=q, nc=nc),
        out_shape=(
            jax.ShapeDtypeStruct((t, D_INNER), BF16),
            jax.ShapeDtypeStruct((nseq, CONV_W - 1, XBC_DIM), F32),
            jax.ShapeDtypeStruct(ssm0_pairs.shape, F32),
        ),
        grid=(nseq, nc),
        in_specs=[
            pl.BlockSpec((q, XBC_DIM), rowblk),
            pl.BlockSpec((q, D_INNER), lambda b, c: (b * nc + c, Z_OFF // D_INNER)),
            pl.BlockSpec((q, LANES), rowblk),
            pl.BlockSpec((1, CONV_W - 1, XBC_DIM), lambda b, c: (b, 0, 0)),
            pl.BlockSpec((1,) + ssm0_pairs.shape[1:], lambda b, c: (b, 0, 0, 0)),
            pl.BlockSpec((CONV_W, XBC_DIM), const),
            pl.BlockSpec((1, XBC_DIM), const),
            pl.BlockSpec((1, LANES), const),
            pl.BlockSpec((1, LANES), const),
            pl.BlockSpec((1, D_INNER), const),
            pl.BlockSpec((1, D_INNER), const),
            pl.BlockSpec((LANES, D_INNER), const),
        ],
        out_specs=(
            pl.BlockSpec((q, D_INNER), rowblk),
            pl.BlockSpec((1, CONV_W - 1, XBC_DIM), lambda b, c: (b, 0, 0)),
            pl.BlockSpec((1,) + ssm0_pairs.shape[1:], lambda b, c: (b, 0, 0, 0)),
        ),
        scratch_shapes=[
            pltpu.VMEM((q + SUBLANES, XBC_DIM), F32),
            pltpu.VMEM((SSD_GROUPS, D_STATE, HEADS_PER_GROUP * SSD_HEAD_DIM), F32),
        ],
        compiler_params=_params(("parallel", "arbitrary"), 48),
        name="ssd",
    )(proj, proj, dt_raw, conv0, ssm0_pairs, lw["conv_w"], lw["conv_b"], lw["dt_bias"], lw["a_log"],
      lw["d_rep"], lw["ssd_norm"], lw["head_rep"])
    return ya, conv_out, ssm_out.reshape(ssm0.shape)


def _gmlp_body(uv_ref, lg_ref, lb_ref, ws_ref, bst_ref, yb_ref, *maybe_vn_ref, tc, nchunks):
    uv = jax.nn.gelu(uv_ref[...])
    u = uv[:, :GM_WIDTH]
    v = uv[:, GM_WIDTH:]
    mu = jnp.mean(v, axis=-1, keepdims=True)
    dev = v - mu
    var = jnp.mean(dev * dev, axis=-1, keepdims=True)
    vn = dev * lax.rsqrt(var + EPS) * lg_ref[...] + lb_ref[...]
    if maybe_vn_ref:
        maybe_vn_ref[0][...] = vn
    vb = vn.astype(BF16)

    row = lax.broadcasted_iota(jnp.int32, (tc, tc), 0)
    col = lax.broadcasted_iota(jnp.int32, (tc, tc), 1)
    tril = row >= col
    mixed = [[None] * GM_GROUPS for _ in range(nchunks)]
    for g in range(GM_GROUPS):
        w = jnp.where(tril, ws_ref[g, :tc, :tc], 0.0).astype(BF16)
        bias = bst_ref[:tc, g:g + 1]
        cols = slice(g * GM_GROUP_DIM, (g + 1) * GM_GROUP_DIM)
        if tc == LANES:
            rhs = jnp.concatenate([vb[r * tc:(r + 1) * tc, cols] for r in range(nchunks)], axis=1)
            out = _dot(w, rhs) + bias
            for r in range(nchunks):
                mixed[r][g] = out[:, r * GM_GROUP_DIM:(r + 1) * GM_GROUP_DIM]
        else:
            for r in range(nchunks):
                mixed[r][g] = _dot(w, vb[r * tc:(r + 1) * tc, cols]) + bias
    full = jnp.concatenate([jnp.concatenate(m, axis=1) for m in mixed], axis=0)
    yb_ref[...] = (u * full).astype(BF16)


def _gmlp(proj, lw, seqlen, keep_v):
    t = proj.shape[0]
    tc = min(GM_CHUNK, seqlen)
    rows = _row_tile(t, 512)
    nchunks = rows // tc
    const2 = lambda i: (0, 0)
    out_shape = [jax.ShapeDtypeStruct((t, GM_WIDTH), BF16)]
    out_specs = [pl.BlockSpec((rows, GM_WIDTH), lambda i: (i, 0))]
    if keep_v:
        out_shape.append(jax.ShapeDtypeStruct((t, GM_WIDTH), F32))
        out_specs.append(pl.BlockSpec((rows, GM_WIDTH), lambda i: (i, 0)))
    outs = pl.pallas_call(
        functools.partial(_gmlp_body, tc=tc, nchunks=nchunks),
        out_shape=tuple(out_shape),
        grid=(t // rows,),
        in_specs=[
            pl.BlockSpec((rows, 2 * GM_WIDTH), lambda i: (i, UV_OFF // (2 * GM_WIDTH))),
            pl.BlockSpec((1, GM_WIDTH), const2),
            pl.BlockSpec((1, GM_WIDTH), const2),
            pl.BlockSpec((GM_GROUPS, GM_CHUNK, GM_CHUNK), lambda i: (0, 0, 0)),
            pl.BlockSpec((GM_CHUNK, GM_GROUPS), const2),
        ],
        out_specs=tuple(out_specs),
        compiler_params=_params(("parallel",), 48),
        name="gmlp",
    )(proj, lw["gm_ln_g"], lw["gm_ln_b"], lw["gm_ws"], lw["gm_bs_t"])
    return (outs[0], outs[1]) if keep_v else (outs[0], None)


def _merge_body(ya_ref, yb_ref, ga_ref, gb_ref, x_ref, wa_ref, wb_ref, wo_ref, o_ref):
    j = pl.program_id(1)

    @pl.when(j == 0)
    def _():
        o_ref[...] = jnp.zeros_like(o_ref)

    merged = (jax.nn.sigmoid(ga_ref[...]) * _dot(ya_ref[...], wa_ref[...])
              + jax.nn.sigmoid(gb_ref[...]) * _dot(yb_ref[...], wb_ref[...]))
    o_ref[...] += _dot(merged.astype(BF16), wo_ref[...])

    @pl.when(j == pl.num_programs(1) - 1)
    def _():
        o_ref[...] = x_ref[...] + o_ref[...]


def _merge(ya, yb, proj, x, wa, wb, wo):
    t, d = x.shape
    tm = _row_tile(t, 512)
    tn = 512
    ga0 = GATE_OFF // tn
    gb0 = (GATE_OFF + d) // tn
    return pl.pallas_call(
        _merge_body,
        out_shape=jax.ShapeDtypeStruct((t, d), F32),
        grid=(t // tm, d // tn),
        in_specs=[
            pl.BlockSpec((tm, D_INNER), lambda i, j: (i, 0)),
            pl.BlockSpec((tm, GM_WIDTH), lambda i, j: (i, 0)),
            pl.BlockSpec((tm, tn), lambda i, j: (i, ga0 + j)),
            pl.BlockSpec((tm, tn), lambda i, j: (i, gb0 + j)),
            pl.BlockSpec((tm, d), lambda i, j: (i, 0)),
            pl.BlockSpec((D_INNER, tn), lambda i, j: (0, j)),
            pl.BlockSpec((GM_WIDTH, tn), lambda i, j: (0, j)),
            pl.BlockSpec((tn, d), lambda i, j: (j, 0)),
        ],
        out_specs=pl.BlockSpec((tm, d), lambda i, j: (i, 0)),
        compiler_params=_params(("parallel", "arbitrary"), 48),
        name="merge",
    )(ya, yb, proj, proj, x, wa, wb, wo)


def _ple_body(x_ref, pe_ref, g_ref, wg_ref, wp_ref, *rest, final):
    o_ref = rest[-1]
    x = x_ref[...]
    h = _rmsnorm(x, g_ref[...]).astype(BF16)
    out = x + _dot(pe_ref[...].astype(BF16), wp_ref[...]) * jax.nn.sigmoid(_dot(h, wg_ref[...]))
    if final:
        out = _rmsnorm(out, rest[0][...])
    o_ref[...] = out


def _ple(x, pe, g, wg, wp, final_g):
    t, d = x.shape
    tm = _row_tile(t, 256)
    final = final_g is not None
    const = lambda i: (0, 0)
    in_specs = [
        pl.BlockSpec((tm, d), lambda i: (i, 0)),
        pl.BlockSpec((tm, pe.shape[1]), lambda i: (i, 0)),
        pl.BlockSpec((1, d), const),
        pl.BlockSpec((d, d), const),
        pl.BlockSpec((pe.shape[1], d), const),
    ]
    args = [x, pe, g, wg, wp]
    if final:
        in_specs.append(pl.BlockSpec((1, d), const))
        args.append(final_g)
    return pl.pallas_call(
        functools.partial(_ple_body, final=final),
        out_shape=jax.ShapeDtypeStruct((t, d), F32),
        grid=(t // tm,),
        in_specs=in_specs,
        out_specs=pl.BlockSpec((tm, d), lambda i: (i, 0)),
        compiler_params=_params(("parallel",), 48),
        name="ple_final" if final else "ple",
    )(*args)


def _layer_weights(i, w):
    w_in = w["w_in"][i]
    o1 = D_INNER
    o2 = o1 + XBC_DIM
    o3 = o2 + SSD_HEADS
    w_packed = jnp.concatenate([w_in[:, o1:o2], w_in[:, :o1], w_in[:, o3:]], axis=1).astype(BF16)
    w_dt = jnp.pad(w_in[:, o2:o3], ((0, 0), (0, LANES - SSD_HEADS))).astype(BF16)
    pad_heads = lambda v: jnp.pad(v[i].reshape(1, SSD_HEADS), ((0, 0), (0, LANES - SSD_HEADS)))
    rowvec = lambda v: v[i].reshape(1, -1)
    head_of_channel = jnp.arange(D_INNER, dtype=jnp.int32) // SSD_HEAD_DIM
    head_rep = (jnp.arange(LANES, dtype=jnp.int32)[:, None] == head_of_channel[None, :]).astype(BF16)
    return {
        "ffn1_norm": rowvec(w["ffn1_norm"]),
        "ffn1": tuple(w[k][i].astype(BF16) for k in ("ffn1_w_gate", "ffn1_w_up", "ffn1_w_down")),
        "mix_norm": rowvec(w["mix_norm"]),
        "w_in": w_packed,
        "w_dt": w_dt,
        "conv_w": w["conv_w"][i],
        "conv_b": rowvec(w["conv_b"]),
        "dt_bias": pad_heads(w["dt_bias"]),
        "a_log": pad_heads(w["a_log"]),
        "d_rep": jnp.repeat(w["d_skip"][i], SSD_HEAD_DIM).reshape(1, D_INNER),
        "ssd_norm": rowvec(w["ssd_norm"]),
        "head_rep": head_rep,
        "gm_ln_g": rowvec(w["gm_ln_g"]),
        "gm_ln_b": rowvec(w["gm_ln_b"]),
        "gm_ws": w["gm_ws"][i],
        "gm_bs_t": w["gm_bs"][i].T,
        "w_a": w["w_branch_ssd"][i].astype(BF16),
        "w_b": w["w_branch_gmlp"][i].astype(BF16),
        "w_out": w["w_out"][i].astype(BF16),
        "ffn2_norm": rowvec(w["ffn2_norm"]),
        "ffn2": tuple(w[k][i].astype(BF16) for k in ("ffn2_w_gate", "ffn2_w_up", "ffn2_w_down")),
        "ple_norm": rowvec(w["ple_norm"]),
        "ple_w_gate": w["ple_w_gate"][i].astype(BF16),
        "ple_w_proj": w["ple_w_proj"][i].astype(BF16),
    }


def _trunk(x, pe, conv_state, ssm_state, layers, final_norm, keep_v):
    nseq, seqlen, d = x.shape
    depth = len(layers)
    t = nseq * seqlen
    x = x.reshape(t, d)
    new_conv, new_ssm, new_v = [], [], []
    for i, lw in enumerate(layers):
        x = _ffn(x, lw["ffn1_norm"], *lw["ffn1"])
        proj, dt_raw = _inproj(x, lw["mix_norm"], lw["w_in"], lw["w_dt"])
        ya, conv_out, ssm_out = _ssd(proj, dt_raw, conv_state[i], ssm_state[i], lw, nseq, seqlen)
        yb, vn = _gmlp(proj, lw, seqlen, keep_v)
        x = _merge(ya, yb, proj, x, lw["w_a"], lw["w_b"], lw["w_out"])
        x = _ffn(x, lw["ffn2_norm"], *lw["ffn2"])
        x = _ple(x, pe[i].reshape(t, -1), lw["ple_norm"], lw["ple_w_gate"], lw["ple_w_proj"],
                 final_norm if i == depth - 1 else None)
        new_conv.append(conv_out)
        new_ssm.append(ssm_out)
        if keep_v:
            new_v.append(vn.reshape(nseq, seqlen, GM_WIDTH))
    y = x.reshape(nseq, seqlen, d)
    return y, jnp.stack(new_conv), jnp.stack(new_ssm), (jnp.stack(new_v) if keep_v else None)


def kernel(x_prompt, x_sample, p_prompt, p_sample, state_conv, state_ssm, ffn1_norm, ffn1_w_gate, ffn1_w_up, ffn1_w_down, mix_norm, w_in, conv_w, conv_b, dt_bias, a_log, d_skip, ssd_norm, gm_ln_g, gm_ln_b, gm_ws, gm_bs, w_branch_ssd, w_branch_gmlp, w_out, ffn2_norm, ffn2_w_gate, ffn2_w_up, ffn2_w_down, ple_norm, ple_w_gate, ple_w_proj, final_norm):
    w = {
        "ffn1_norm": ffn1_norm, "ffn1_w_gate": ffn1_w_gate, "ffn1_w_up": ffn1_w_up, "ffn1_w_down": ffn1_w_down,
        "mix_norm": mix_norm, "w_in": w_in, "conv_w": conv_w, "conv_b": conv_b,
        "dt_bias": dt_bias, "a_log": a_log, "d_skip": d_skip, "ssd_norm": ssd_norm,
        "gm_ln_g": gm_ln_g, "gm_ln_b": gm_ln_b, "gm_ws": gm_ws, "gm_bs": gm_bs,
        "w_branch_ssd": w_branch_ssd, "w_branch_gmlp": w_branch_gmlp, "w_out": w_out,
        "ffn2_norm": ffn2_norm, "ffn2_w_gate": ffn2_w_gate, "ffn2_w_up": ffn2_w_up, "ffn2_w_down": ffn2_w_down,
        "ple_norm": ple_norm, "ple_w_gate": ple_w_gate, "ple_w_proj": ple_w_proj,
    }
    depth = w_in.shape[0]
    layers = [_layer_weights(i, w) for i in range(depth)]
    fnorm = final_norm.reshape(1, -1)
    bp = x_prompt.shape[0]
    conv0 = jnp.zeros((depth, bp, CONV_W - 1, XBC_DIM), F32)
    ssm0 = jnp.zeros((depth, bp, SSD_HEADS, SSD_HEAD_DIM, D_STATE), F32)
    y_prompt, conv_prompt, ssm_prompt, _ = _trunk(x_prompt, p_prompt, conv0, ssm0, layers, fnorm, False)
    y_sample, conv_sample, ssm_sample, v_sample = _trunk(x_sample, p_sample, state_conv, state_ssm, layers, fnorm, True)
    return (y_prompt, y_sample, ssm_prompt, conv_prompt, ssm_sample, conv_sample, v_sample)
```

```python
import functools

import jax
import jax.numpy as jnp
from jax import lax
from jax.experimental import pallas as pl
from jax.experimental.pallas import tpu as pltpu

F32 = jnp.float32
BF16 = jnp.bfloat16

D_MODEL = 2048
SSD_HEADS = 32
SSD_HEAD_DIM = 64
D_INNER = SSD_HEADS * SSD_HEAD_DIM
SSD_GROUPS = 4
HEADS_PER_GROUP = SSD_HEADS // SSD_GROUPS
D_STATE = 128
CONV_W = 4
BC_DIM = SSD_GROUPS * D_STATE
XBC_DIM = D_INNER + 2 * BC_DIM
GM_CHUNK = 128
GM_GROUPS = 8
GM_GROUP_DIM = 128
GM_WIDTH = GM_GROUPS * GM_GROUP_DIM
EPS = 1e-6

LANES = 128
SUBLANES = 8
MIB = 1024 * 1024

PROJ_TN = 1024
XBC_OFF = 0
Z_OFF = 4096
UV_OFF = 6144
GATE_OFF = 8192
PROJ_COLS = 12288
PROJ_PAD_TILE = XBC_DIM // PROJ_TN
PROJ_W_COLS = XBC_DIM + D_INNER + 2 * GM_WIDTH + 2 * D_MODEL

SSD_CHUNK = 128
CONV_TAIL_ROW = SUBLANES - (CONV_W - 1)


def _params(semantics, vmem_mib):
    return pltpu.CompilerParams(dimension_semantics=semantics, vmem_limit_bytes=vmem_mib * MIB)


def _rmsnorm(x, g):
    return x * lax.rsqrt(jnp.mean(x * x, axis=-1, keepdims=True) + EPS) * g


def _dot(a, b):
    return jnp.dot(a, b, preferred_element_type=F32)


def _row_tile(t, want):
    return want if t % want == 0 else t


def _ffn_body(x_ref, g_ref, wg_ref, wu_ref, wd_ref, o_ref, h_ref):
    j = pl.program_id(1)

    @pl.when(j == 0)
    def _():
        x = x_ref[...]
        h_ref[...] = _rmsnorm(x, g_ref[...]).astype(BF16)
        o_ref[...] = x

    h = h_ref[...]
    act = (0.5 * jax.nn.silu(_dot(h, wg_ref[...])) * _dot(h, wu_ref[...])).astype(BF16)
    o_ref[...] += _dot(act, wd_ref[...])


def _ffn(x, g, wg, wu, wd, layer):
    t, d = x.shape
    dff = wg.shape[-1]
    tm = _row_tile(t, 1024)
    tf = 512
    return pl.pallas_call(
        _ffn_body,
        out_shape=jax.ShapeDtypeStruct((t, d), F32),
        grid=(t // tm, dff // tf),
        in_specs=[
            pl.BlockSpec((tm, d), lambda i, j: (i, 0)),
            pl.BlockSpec((1, d), lambda i, j: (0, 0)),
            pl.BlockSpec((None, d, tf), lambda i, j: (layer, 0, j)),
            pl.BlockSpec((None, d, tf), lambda i, j: (layer, 0, j)),
            pl.BlockSpec((None, tf, d), lambda i, j: (layer, j, 0)),
        ],
        out_specs=pl.BlockSpec((tm, d), lambda i, j: (i, 0)),
        scratch_shapes=[pltpu.VMEM((tm, d), BF16)],
        compiler_params=_params(("parallel", "arbitrary"), 58),
        name="ffn",
    )(x, g, wg, wu, wd)


def _inproj_body(x_ref, g_ref, w_ref, wdt_ref, o_ref, dt_ref, h_ref):
    @pl.when(pl.program_id(1) == 0)
    def _():
        h = _rmsnorm(x_ref[...], g_ref[...]).astype(BF16)
        h_ref[...] = h
        dt_ref[...] = _dot(h, wdt_ref[...])

    o_ref[...] = _dot(h_ref[...], w_ref[...])


def _inproj(x, g, w, wdt, layer):
    t, d = x.shape
    tm = _row_tile(t, 1024)
    n_tiles = PROJ_W_COLS // PROJ_TN
    return pl.pallas_call(
        _inproj_body,
        out_shape=(jax.ShapeDtypeStruct((t, PROJ_COLS), F32), jax.ShapeDtypeStruct((t, LANES), F32)),
        grid=(t // tm, n_tiles),
        in_specs=[
            pl.BlockSpec((tm, d), lambda i, j: (i, 0)),
            pl.BlockSpec((1, d), lambda i, j: (0, 0)),
            pl.BlockSpec((None, d, PROJ_TN), lambda i, j: (layer, 0, j)),
            pl.BlockSpec((None, d, LANES), lambda i, j: (layer, 0, 0)),
        ],
        out_specs=(
            pl.BlockSpec((tm, PROJ_TN), lambda i, j: (i, jnp.where(j >= PROJ_PAD_TILE, j + 1, j))),
            pl.BlockSpec((tm, LANES), lambda i, j: (i, 0)),
        ),
        scratch_shapes=[pltpu.VMEM((tm, d), BF16)],
        compiler_params=_params(("parallel", "arbitrary"), 52),
        name="inproj",
    )(x, g, w, wdt)


def _split_bf16(v, n):
    parts = []
    rest = v
    for _ in range(n):
        p = rest.astype(BF16)
        parts.append(p)
        rest = rest - p.astype(F32)
    return parts


def _dot_split(a_f32, b_bf16, n):
    parts = _split_bf16(a_f32, n)
    out = _dot(parts[0], b_bf16)
    for p in parts[1:]:
        out = out + _dot(p, b_bf16)
    return out


def _transpose_rows_to_lanes(x):
    q = x.shape[0]
    if q == LANES:
        return x.T
    pad = jnp.zeros((LANES - q, x.shape[1]), x.dtype)
    return jnp.concatenate([x, pad], axis=0).T[:, :q]


def _ssd_body(xbc_ref, z_ref, dt_ref, conv0_ref, ssm0_ref, cw_ref, cb_ref, dtb_ref, alog_ref, drep_ref,
              ng_ref, rep_ref, ya_ref, convo_ref, ssmo_ref, xp_ref, st_ref, *, q, nc):
    c = pl.program_id(1)
    pairs_per_group = HEADS_PER_GROUP // 2
    gw = HEADS_PER_GROUP * SSD_HEAD_DIM

    @pl.when(c == 0)
    def _():
        xp_ref[CONV_TAIL_ROW:SUBLANES, :] = conv0_ref[0]
        for k in range(SSD_HEADS // 2):
            g, kk = divmod(k, pairs_per_group)
            st_ref[g, :, kk * LANES:(kk + 1) * LANES] = ssm0_ref[0, k].T

    xp_ref[SUBLANES:SUBLANES + q, :] = xbc_ref[...]
    acc = cb_ref[...]
    for k in range(CONV_W):
        acc = acc + xp_ref[CONV_TAIL_ROW + k:CONV_TAIL_ROW + k + q, :] * cw_ref[k:k + 1, :]
    xc = jax.nn.silu(acc)

    @pl.when(c == nc - 1)
    def _():
        convo_ref[0] = xp_ref[q + CONV_TAIL_ROW:q + SUBLANES, :]

    xp_ref[CONV_TAIL_ROW:SUBLANES, :] = xp_ref[q + CONV_TAIL_ROW:q + SUBLANES, :]

    xs = xc[:, :D_INNER]

    dt = jax.nn.softplus(dt_ref[...] + dtb_ref[...])
    d_a = dt * (-jnp.exp(alog_ref[...]))
    row = lax.broadcasted_iota(jnp.int32, (q, q), 0)
    col = lax.broadcasted_iota(jnp.int32, (q, q), 1)
    tril = row >= col
    cum = _dot_split_lhs01(jnp.where(tril, 1.0, 0.0).astype(BF16), d_a, 3)
    cum_last = cum[q - 1:q, :]
    ecum = jnp.exp(cum)
    dec_end = jnp.exp(cum_last - cum)
    cdec = jnp.broadcast_to(jnp.exp(cum_last), (SUBLANES, LANES))

    stacked = jnp.concatenate([dt, ecum, dec_end, cdec], axis=0)
    expanded = _dot_split(stacked, rep_ref[...], 2)
    dt_e = expanded[:q]
    ecum_e = expanded[q:2 * q]
    dec_e = expanded[2 * q:3 * q]
    cdec_e = expanded[3 * q:3 * q + 1]

    xdt = xs * dt_e
    xdt_b = xdt.astype(BF16)
    xdtd_b = (xdt * dec_e).astype(BF16)
    cum_t = _transpose_rows_to_lanes(cum)
    left = lax.broadcasted_iota(jnp.int32, (q, LANES), 1) < SSD_HEAD_DIM
    zero_b = jnp.zeros((q, LANES), BF16)

    def masked_decay(cb, h):
        seg = cum[:, h:h + 1] - cum_t[h:h + 1, :]
        return (cb * jnp.exp(jnp.where(tril, seg, -jnp.inf))).astype(BF16)

    ys = []
    for g in range(SSD_GROUPS):
        b_f32 = xc[:, D_INNER + g * D_STATE:D_INNER + (g + 1) * D_STATE]
        b_g = b_f32.astype(BF16)
        c_g = xc[:, D_INNER + BC_DIM + g * D_STATE:D_INNER + BC_DIM + (g + 1) * D_STATE].astype(BF16)
        cb = lax.dot_general(c_g, b_g, (((1,), (1,)), ((), ())), preferred_element_type=F32)
        st_g = st_ref[g]
        y_off = _dot(c_g, st_g.astype(BF16)) * ecum_e[:, g * gw:(g + 1) * gw]
        for kk in range(pairs_per_group):
            pair = g * pairs_per_group + kk
            m0 = masked_decay(cb, 2 * pair)
            m1 = masked_decay(cb, 2 * pair + 1)
            xpair = xdt_b[:, pair * LANES:(pair + 1) * LANES]
            x_l = jnp.where(left, xpair, zero_b)
            x_r = jnp.where(left, zero_b, xpair)
            if q % LANES == 0:
                y_diag = _dot(jnp.concatenate([m0, m1], axis=1), jnp.concatenate([x_l, x_r], axis=0))
            else:
                y_diag = _dot(m0, x_l) + _dot(m1, x_r)
            ys.append(y_diag + y_off[:, kk * LANES:(kk + 1) * LANES])
        b_t = _transpose_rows_to_lanes(b_f32).astype(BF16)
        st_ref[g] = st_g * cdec_e[:, g * gw:(g + 1) * gw] + _dot(b_t, xdtd_b[:, g * gw:(g + 1) * gw])

    y = jnp.concatenate(ys, axis=1) + xs * drep_ref[...]

    yg = y * jax.nn.silu(z_ref[...])
    normed = []
    for g in range(SSD_GROUPS):
        blk = yg[:, g * gw:(g + 1) * gw]
        normed.append(blk * lax.rsqrt(jnp.mean(blk * blk, axis=-1, keepdims=True) + EPS))
    ya_ref[...] = (jnp.concatenate(normed, axis=1) * ng_ref[...]).astype(BF16)

    @pl.when(c == nc - 1)
    def _():
        for k in range(SSD_HEADS // 2):
            g, kk = divmod(k, pairs_per_group)
            ssmo_ref[0, k] = st_ref[g, :, kk * LANES:(kk + 1) * LANES].T


def _dot_split_lhs01(lhs01_bf16, rhs_f32, n):
    parts = _split_bf16(rhs_f32, n)
    out = _dot(lhs01_bf16, parts[0])
    for p in parts[1:]:
        out = out + _dot(lhs01_bf16, p)
    return out


def _ssd(proj, dt_raw, conv0, ssm0, lw, nseq, seqlen):
    t = proj.shape[0]
    q = min(SSD_CHUNK, seqlen)
    nc = seqlen // q
    ssm0_pairs = ssm0.reshape(nseq, SSD_HEADS // 2, 2 * SSD_HEAD_DIM, D_STATE)
    const = lambda b, c: (0, 0)
    rowblk = lambda b, c: (b * nc + c, 0)
    ya, conv_out, ssm_out = pl.pallas_call(
        functools.partial(_ssd_body, q=q, nc=nc),
        out_shape=(
            jax.ShapeDtypeStruct((t, D_INNER), BF16),
            jax.ShapeDtypeStruct((nseq, CONV_W - 1, XBC_DIM), F32),
            jax.ShapeDtypeStruct(ssm0_pairs.shape, F32),
        ),
        grid=(nseq, nc),
        in_specs=[
            pl.BlockSpec((q, XBC_DIM), rowblk),
            pl.BlockSpec((q, D_INNER), lambda b, c: (b * nc + c, Z_OFF // D_INNER)),
            pl.BlockSpec((q, LANES), rowblk),
            pl.BlockSpec((1, CONV_W - 1, XBC_DIM), lambda b, c: (b, 0, 0)),
            pl.BlockSpec((1,) + ssm0_pairs.shape[1:], lambda b, c: (b, 0, 0, 0)),
            pl.BlockSpec((CONV_W, XBC_DIM), const),
            pl.BlockSpec((1, XBC_DIM), const),
            pl.BlockSpec((1, LANES), const),
            pl.BlockSpec((1, LANES), const),
            pl.BlockSpec((1, D_INNER), const),
            pl.BlockSpec((1, D_INNER), const),
            pl.BlockSpec((LANES, D_INNER), const),
        ],
        out_specs=(
            pl.BlockSpec((q, D_INNER), rowblk),
            pl.BlockSpec((1, CONV_W - 1, XBC_DIM), lambda b, c: (b, 0, 0)),
            pl.BlockSpec((1,) + ssm0_pairs.shape[1:], lambda b, c: (b, 0, 0, 0)),
        ),
        scratch_shapes=[
            pltpu.VMEM((q + SUBLANES, XBC_DIM), F32),
            pltpu.VMEM((SSD_GROUPS, D_STATE, HEADS_PER_GROUP * SSD_HEAD_DIM), F32),
        ],
        compiler_params=_params(("parallel", "arbitrary"), 48),
        name="ssd",
    )(proj, proj, dt_raw, conv0, ssm0_pairs, lw["conv_w"], lw["conv_b"], lw["dt_bias"], lw["a_log"],
      lw["d_rep"], lw["ssd_norm"], lw["head_rep"])
    return ya, conv_out, ssm_out.reshape(ssm0.shape)


def _gmlp_body(uv_ref, lg_ref, lb_ref, ws_ref, bst_ref, yb_ref, *maybe_vn_ref, tc, nchunks):
    uv = jax.nn.gelu(uv_ref[...])
    u = uv[:, :GM_WIDTH]
    v = uv[:, GM_WIDTH:]
    mu = jnp.mean(v, axis=-1, keepdims=True)
    dev = v - mu
    var = jnp.mean(dev * dev, axis=-1, keepdims=True)
    vn = dev * lax.rsqrt(var + EPS) * lg_ref[...] + lb_ref[...]
    if maybe_vn_ref:
        maybe_vn_ref[0][...] = vn
    vb = vn.astype(BF16)

    row = lax.broadcasted_iota(jnp.int32, (tc, tc), 0)
    col = lax.broadcasted_iota(jnp.int32, (tc, tc), 1)
    tril = row >= col
    mixed = [[None] * GM_GROUPS for _ in range(nchunks)]
    for g in range(GM_GROUPS):
        w = jnp.where(tril, ws_ref[g, :tc, :tc], 0.0).astype(BF16)
        bias = bst_ref[:tc, g:g + 1]
        cols = slice(g * GM_GROUP_DIM, (g + 1) * GM_GROUP_DIM)
        if tc == LANES:
            rhs = jnp.concatenate([vb[r * tc:(r + 1) * tc, cols] for r in range(nchunks)], axis=1)
            out = _dot(w, rhs) + bias
            for r in range(nchunks):
                mixed[r][g] = out[:, r * GM_GROUP_DIM:(r + 1) * GM_GROUP_DIM]
        else:
            for r in range(nchunks):
                mixed[r][g] = _dot(w, vb[r * tc:(r + 1) * tc, cols]) + bias
    full = jnp.concatenate([jnp.concatenate(m, axis=1) for m in mixed], axis=0)
    yb_ref[...] = (u * full).astype(BF16)


def _gmlp(proj, lw, seqlen, keep_v):
    t = proj.shape[0]
    tc = min(GM_CHUNK, seqlen)
    rows = _row_tile(t, 512)
    nchunks = rows // tc
    const2 = lambda i: (0, 0)
    out_shape = [jax.ShapeDtypeStruct((t, GM_WIDTH), BF16)]
    out_specs = [pl.BlockSpec((rows, GM_WIDTH), lambda i: (i, 0))]
    if keep_v:
        out_shape.append(jax.ShapeDtypeStruct((t, GM_WIDTH), F32))
        out_specs.append(pl.BlockSpec((rows, GM_WIDTH), lambda i: (i, 0)))
    outs = pl.pallas_call(
        functools.partial(_gmlp_body, tc=tc, nchunks=nchunks),
        out_shape=tuple(out_shape),
        grid=(t // rows,),
        in_specs=[
            pl.BlockSpec((rows, 2 * GM_WIDTH), lambda i: (i, UV_OFF // (2 * GM_WIDTH))),
            pl.BlockSpec((1, GM_WIDTH), const2),
            pl.BlockSpec((1, GM_WIDTH), const2),
            pl.BlockSpec((GM_GROUPS, GM_CHUNK, GM_CHUNK), lambda i: (0, 0, 0)),
            pl.BlockSpec((GM_CHUNK, GM_GROUPS), const2),
        ],
        out_specs=tuple(out_specs),
        compiler_params=_params(("parallel",), 48),
        name="gmlp",
    )(proj, lw["gm_ln_g"], lw["gm_ln_b"], lw["gm_ws"], lw["gm_bs_t"])
    return (outs[0], outs[1]) if keep_v else (outs[0], None)


def _merge_body(ya_ref, yb_ref, ga_ref, gb_ref, x_ref, wa_ref, wb_ref, wo_ref, o_ref):
    merged = (jax.nn.sigmoid(ga_ref[...]) * _dot(ya_ref[...], wa_ref[...])
              + jax.nn.sigmoid(gb_ref[...]) * _dot(yb_ref[...], wb_ref[...]))
    o_ref[...] = x_ref[...] + _dot(merged.astype(BF16), wo_ref[...])


def _resident(block_shape, index_map):
    return pl.BlockSpec(block_shape, index_map, pipeline_mode=pl.Buffered(1))


def _merge(ya, yb, proj, x, wa, wb, wo, layer):
    t, d = x.shape
    tm = _row_tile(t, 256)
    ga = GATE_OFF // d
    whole = lambda i: (layer, 0, 0)
    return pl.pallas_call(
        _merge_body,
        out_shape=jax.ShapeDtypeStruct((t, d), F32),
        grid=(t // tm,),
        in_specs=[
            pl.BlockSpec((tm, D_INNER), lambda i: (i, 0)),
            pl.BlockSpec((tm, GM_WIDTH), lambda i: (i, 0)),
            pl.BlockSpec((tm, d), lambda i: (i, ga)),
            pl.BlockSpec((tm, d), lambda i: (i, ga + 1)),
            pl.BlockSpec((tm, d), lambda i: (i, 0)),
            _resident((None, D_INNER, d), whole),
            _resident((None, GM_WIDTH, d), whole),
            _resident((None, d, d), whole),
        ],
        out_specs=pl.BlockSpec((tm, d), lambda i: (i, 0)),
        compiler_params=_params(("parallel",), 52),
        name="merge",
    )(ya, yb, proj, proj, x, wa, wb, wo)


def _ple_body(x_ref, pe_ref, g_ref, wg_ref, wp_ref, *rest, final):
    o_ref = rest[-1]
    x = x_ref[...]
    h = _rmsnorm(x, g_ref[...]).astype(BF16)
    out = x + _dot(pe_ref[...].astype(BF16), wp_ref[...]) * jax.nn.sigmoid(_dot(h, wg_ref[...]))
    if final:
        out = _rmsnorm(out, rest[0][...])
    o_ref[...] = out


def _ple(x, pe, g, wg, wp, final_g, layer):
    t, d = x.shape
    tm = _row_tile(t, 512)
    final = final_g is not None
    const = lambda i: (0, 0)
    whole = lambda i: (layer, 0, 0)
    in_specs = [
        pl.BlockSpec((tm, d), lambda i: (i, 0)),
        pl.BlockSpec((tm, pe.shape[1]), lambda i: (i, 0)),
        pl.BlockSpec((1, d), const),
        _resident((None, d, d), whole),
        _resident((None, pe.shape[1], d), whole),
    ]
    args = [x, pe, g, wg, wp]
    if final:
        in_specs.append(pl.BlockSpec((1, d), const))
        args.append(final_g)
    return pl.pallas_call(
        functools.partial(_ple_body, final=final),
        out_shape=jax.ShapeDtypeStruct((t, d), F32),
        grid=(t // tm,),
        in_specs=in_specs,
        out_specs=pl.BlockSpec((tm, d), lambda i: (i, 0)),
        compiler_params=_params(("parallel",), 48),
        name="ple_final" if final else "ple",
    )(*args)


def _stacked_weights(w):
    w_in = w["w_in"]
    o1 = D_INNER
    o2 = o1 + XBC_DIM
    o3 = o2 + SSD_HEADS
    stacks = {k: w[k].astype(BF16) for k in (
        "ffn1_w_gate", "ffn1_w_up", "ffn1_w_down", "ffn2_w_gate", "ffn2_w_up", "ffn2_w_down",
        "w_branch_ssd", "w_branch_gmlp", "w_out", "ple_w_gate", "ple_w_proj")}
    stacks["w_in"] = jnp.concatenate([w_in[..., o1:o2], w_in[..., :o1], w_in[..., o3:]], axis=-1).astype(BF16)
    stacks["w_dt"] = jnp.pad(w_in[..., o2:o3], ((0, 0), (0, 0), (0, LANES - SSD_HEADS))).astype(BF16)
    return stacks


def _layer_weights(i, w):
    pad_heads = lambda v: jnp.pad(v[i].reshape(1, SSD_HEADS), ((0, 0), (0, LANES - SSD_HEADS)))
    rowvec = lambda v: v[i].reshape(1, -1)
    head_of_channel = jnp.arange(D_INNER, dtype=jnp.int32) // SSD_HEAD_DIM
    head_rep = (jnp.arange(LANES, dtype=jnp.int32)[:, None] == head_of_channel[None, :]).astype(BF16)
    return {
        "ffn1_norm": rowvec(w["ffn1_norm"]),
        "mix_norm": rowvec(w["mix_norm"]),
        "conv_w": w["conv_w"][i],
        "conv_b": rowvec(w["conv_b"]),
        "dt_bias": pad_heads(w["dt_bias"]),
        "a_log": pad_heads(w["a_log"]),
        "d_rep": jnp.repeat(w["d_skip"][i], SSD_HEAD_DIM).reshape(1, D_INNER),
        "ssd_norm": rowvec(w["ssd_norm"]),
        "head_rep": head_rep,
        "gm_ln_g": rowvec(w["gm_ln_g"]),
        "gm_ln_b": rowvec(w["gm_ln_b"]),
        "gm_ws": w["gm_ws"][i],
        "gm_bs_t": w["gm_bs"][i].T,
        "ffn2_norm": rowvec(w["ffn2_norm"]),
        "ple_norm": rowvec(w["ple_norm"]),
    }


def _trunk(x, pe, conv_state, ssm_state, sw, layers, final_norm, keep_v):
    nseq, seqlen, d = x.shape
    depth = len(layers)
    t = nseq * seqlen
    x = x.reshape(t, d)
    new_conv, new_ssm, new_v = [], [], []
    for i, lw in enumerate(layers):
        x = _ffn(x, lw["ffn1_norm"], sw["ffn1_w_gate"], sw["ffn1_w_up"], sw["ffn1_w_down"], i)
        proj, dt_raw = _inproj(x, lw["mix_norm"], sw["w_in"], sw["w_dt"], i)
        ya, conv_out, ssm_out = _ssd(proj, dt_raw, conv_state[i], ssm_state[i], lw, nseq, seqlen)
        yb, vn = _gmlp(proj, lw, seqlen, keep_v)
        x = _merge(ya, yb, proj, x, sw["w_branch_ssd"], sw["w_branch_gmlp"], sw["w_out"], i)
        x = _ffn(x, lw["ffn2_norm"], sw["ffn2_w_gate"], sw["ffn2_w_up"], sw["ffn2_w_down"], i)
        x = _ple(x, pe[i].reshape(t, -1), lw["ple_norm"], sw["ple_w_gate"], sw["ple_w_proj"],
                 final_norm if i == depth - 1 else None, i)
        new_conv.append(conv_out)
        new_ssm.append(ssm_out)
        if keep_v:
            new_v.append(vn.reshape(nseq, seqlen, GM_WIDTH))
    y = x.reshape(nseq, seqlen, d)
    return y, jnp.stack(new_conv), jnp.stack(new_ssm), (jnp.stack(new_v) if keep_v else None)


def kernel(x_prompt, x_sample, p_prompt, p_sample, state_conv, state_ssm, ffn1_norm, ffn1_w_gate, ffn1_w_up, ffn1_w_down, mix_norm, w_in, conv_w, conv_b, dt_bias, a_log, d_skip, ssd_norm, gm_ln_g, gm_ln_b, gm_ws, gm_bs, w_branch_ssd, w_branch_gmlp, w_out, ffn2_norm, ffn2_w_gate, ffn2_w_up, ffn2_w_down, ple_norm, ple_w_gate, ple_w_proj, final_norm):
    w = {
        "ffn1_norm": ffn1_norm, "ffn1_w_gate": ffn1_w_gate, "ffn1_w_up": ffn1_w_up, "ffn1_w_down": ffn1_w_down,
        "mix_norm": mix_norm, "w_in": w_in, "conv_w": conv_w, "conv_b": conv_b,
        "dt_bias": dt_bias, "a_log": a_log, "d_skip": d_skip, "ssd_norm": ssd_norm,
        "gm_ln_g": gm_ln_g, "gm_ln_b": gm_ln_b, "gm_ws": gm_ws, "gm_bs": gm_bs,
        "w_branch_ssd": w_branch_ssd, "w_branch_gmlp": w_branch_gmlp, "w_out": w_out,
        "ffn2_norm": ffn2_norm, "ffn2_w_gate": ffn2_w_gate, "ffn2_w_up": ffn2_w_up, "ffn2_w_down": ffn2_w_down,
        "ple_norm": ple_norm, "ple_w_gate": ple_w_gate, "ple_w_proj": ple_w_proj,
    }
    depth = w_in.shape[0]
    layers = [_layer_weights(i, w) for i in range(depth)]
    sw = _stacked_weights(w)
    fnorm = final_norm.reshape(1, -1)
    bp = x_prompt.shape[0]
    conv0 = jnp.zeros((depth, bp, CONV_W - 1, XBC_DIM), F32)
    ssm0 = jnp.zeros((depth, bp, SSD_HEADS, SSD_HEAD_DIM, D_STATE), F32)
    y_prompt, conv_prompt, ssm_prompt, _ = _trunk(x_prompt, p_prompt, conv0, ssm0, sw, layers, fnorm, False)
    y_sample, conv_sample, ssm_sample, v_sample = _trunk(x_sample, p_sample, state_conv, state_ssm, sw, layers, fnorm, True)
    return (y_prompt, y_sample, ssm_prompt, conv_prompt, ssm_sample, conv_sample, v_sample)
```

```python
import functools

import jax
import jax.numpy as jnp
from jax import lax
from jax.experimental import pallas as pl
from jax.experimental.pallas import tpu as pltpu

F32 = jnp.float32
BF16 = jnp.bfloat16

D_MODEL = 2048
SSD_HEADS = 32
SSD_HEAD_DIM = 64
D_INNER = SSD_HEADS * SSD_HEAD_DIM
SSD_GROUPS = 4
HEADS_PER_GROUP = SSD_HEADS // SSD_GROUPS
D_STATE = 128
CONV_W = 4
BC_DIM = SSD_GROUPS * D_STATE
XBC_DIM = D_INNER + 2 * BC_DIM
GM_CHUNK = 128
GM_GROUPS = 8
GM_GROUP_DIM = 128
GM_WIDTH = GM_GROUPS * GM_GROUP_DIM
EPS = 1e-6

LANES = 128
SUBLANES = 8
MIB = 1024 * 1024

PROJ_TN = 1024
XBC_OFF = 0
Z_OFF = 4096
UV_OFF = 6144
GATE_OFF = 8192
PROJ_COLS = 12288
PROJ_PAD_TILE = XBC_DIM // PROJ_TN
PROJ_W_COLS = XBC_DIM + D_INNER + 2 * GM_WIDTH + 2 * D_MODEL

SSD_CHUNK = 128
BF16_ROWS = 2 * SUBLANES
CONV_PAD_ROWS = BF16_ROWS


def _params(semantics, vmem_mib):
    return pltpu.CompilerParams(dimension_semantics=semantics, vmem_limit_bytes=vmem_mib * MIB)


def _rmsnorm(x, g):
    return x * lax.rsqrt(jnp.mean(x * x, axis=-1, keepdims=True) + EPS) * g


def _dot(a, b):
    return jnp.dot(a, b, preferred_element_type=F32)


def _row_tile(t, want):
    return want if t % want == 0 else t


def _ffn_body(x_ref, g_ref, wg_ref, wu_ref, wd_ref, o_ref, h_ref):
    j = pl.program_id(1)

    @pl.when(j == 0)
    def _():
        x = x_ref[...]
        h_ref[...] = _rmsnorm(x, g_ref[...]).astype(BF16)
        o_ref[...] = x

    h = h_ref[...]
    act = (0.5 * jax.nn.silu(_dot(h, wg_ref[...])) * _dot(h, wu_ref[...])).astype(BF16)
    o_ref[...] += _dot(act, wd_ref[...])


def _ffn(x, g, wg, wu, wd, layer):
    t, d = x.shape
    dff = wg.shape[-1]
    tm = _row_tile(t, 1024)
    tf = 512
    return pl.pallas_call(
        _ffn_body,
        out_shape=jax.ShapeDtypeStruct((t, d), F32),
        grid=(t // tm, dff // tf),
        in_specs=[
            pl.BlockSpec((tm, d), lambda i, j: (i, 0)),
            pl.BlockSpec((1, d), lambda i, j: (0, 0)),
            pl.BlockSpec((None, d, tf), lambda i, j: (layer, 0, j)),
            pl.BlockSpec((None, d, tf), lambda i, j: (layer, 0, j)),
            pl.BlockSpec((None, tf, d), lambda i, j: (layer, j, 0)),
        ],
        out_specs=pl.BlockSpec((tm, d), lambda i, j: (i, 0)),
        scratch_shapes=[pltpu.VMEM((tm, d), BF16)],
        compiler_params=_params(("parallel", "arbitrary"), 58),
        name="ffn",
    )(x, g, wg, wu, wd)


def _inproj_body(x_ref, g_ref, w_ref, wdt_ref, o_ref, dt_ref, h_ref):
    @pl.when(pl.program_id(1) == 0)
    def _():
        h = _rmsnorm(x_ref[...], g_ref[...]).astype(BF16)
        h_ref[...] = h
        dt_ref[...] = _dot(h, wdt_ref[...])

    o_ref[...] = _dot(h_ref[...], w_ref[...]).astype(o_ref.dtype)


def _inproj(x, g, w, wdt, layer):
    t, d = x.shape
    tm = _row_tile(t, 1024)
    n_tiles = PROJ_W_COLS // PROJ_TN
    return pl.pallas_call(
        _inproj_body,
        out_shape=(jax.ShapeDtypeStruct((t, PROJ_COLS), BF16), jax.ShapeDtypeStruct((t, LANES), F32)),
        grid=(t // tm, n_tiles),
        in_specs=[
            pl.BlockSpec((tm, d), lambda i, j: (i, 0)),
            pl.BlockSpec((1, d), lambda i, j: (0, 0)),
            pl.BlockSpec((None, d, PROJ_TN), lambda i, j: (layer, 0, j)),
            pl.BlockSpec((None, d, LANES), lambda i, j: (layer, 0, 0)),
        ],
        out_specs=(
            pl.BlockSpec((tm, PROJ_TN), lambda i, j: (i, jnp.where(j >= PROJ_PAD_TILE, j + 1, j))),
            pl.BlockSpec((tm, LANES), lambda i, j: (i, 0)),
        ),
        scratch_shapes=[pltpu.VMEM((tm, d), BF16)],
        compiler_params=_params(("parallel", "arbitrary"), 52),
        name="inproj",
    )(x, g, w, wdt)


def _split_bf16(v, n):
    parts = []
    rest = v
    for _ in range(n):
        p = rest.astype(BF16)
        parts.append(p)
        rest = rest - p.astype(F32)
    return parts


def _dot_split(a_f32, b_bf16, n):
    parts = _split_bf16(a_f32, n)
    out = _dot(parts[0], b_bf16)
    for p in parts[1:]:
        out = out + _dot(p, b_bf16)
    return out


def _transpose_rows_to_lanes(x):
    q = x.shape[0]
    if q == LANES:
        return x.T
    pad = jnp.zeros((LANES - q, x.shape[1]), x.dtype)
    return jnp.concatenate([x, pad], axis=0).T[:, :q]


def _ssd_body(xbc_ref, z_ref, dt_ref, conv0_ref, ssm0_ref, cw_ref, cb_ref, dtb_ref, alog_ref, drep_ref,
              ng_ref, rep_ref, ya_ref, convo_ref, ssmo_ref, prev_ref, tail_ref, st_ref, *, q, nc):
    c = pl.program_id(1)
    pairs_per_group = HEADS_PER_GROUP // 2
    gw = HEADS_PER_GROUP * SSD_HEAD_DIM

    @pl.when(c == 0)
    def _():
        prev_ref[...] = jnp.zeros_like(prev_ref)
        prev_ref[q - CONV_PAD_ROWS:q, :] = conv0_ref[0].astype(BF16)
        for k in range(SSD_HEADS // 2):
            g, kk = divmod(k, pairs_per_group)
            st_ref[g, :, kk * LANES:(kk + 1) * LANES] = ssm0_ref[0, k].T

    cur = xbc_ref[...]
    both = jnp.concatenate([prev_ref[...], cur], axis=0)
    srow = lax.broadcasted_iota(jnp.int32, (q, 2 * q), 0)
    scol = lax.broadcasted_iota(jnp.int32, (q, 2 * q), 1)
    shift_mats = [jnp.where(scol == srow + (q - (CONV_W - 1) + k), 1.0, 0.0).astype(BF16)
                  for k in range(CONV_W - 1)]
    shifted = _dot(jnp.concatenate(shift_mats, axis=0), both)
    acc = cb_ref[...]
    for k in range(CONV_W - 1):
        acc = acc + shifted[k * q:(k + 1) * q] * cw_ref[k:k + 1, :]
    acc = acc + cur.astype(F32) * cw_ref[CONV_W - 1:CONV_W, :]
    xc = jax.nn.silu(acc)

    @pl.when(c == nc - 1)
    def _():
        tail_ref[...] = cur[q - CONV_PAD_ROWS:q].astype(F32)
        convo_ref[0] = tail_ref[CONV_PAD_ROWS - (CONV_W - 1):CONV_PAD_ROWS, :]

    prev_ref[...] = cur

    xs = xc[:, :D_INNER]

    dt = jax.nn.softplus(dt_ref[...] + dtb_ref[...])
    d_a = dt * (-jnp.exp(alog_ref[...]))
    row = lax.broadcasted_iota(jnp.int32, (q, q), 0)
    col = lax.broadcasted_iota(jnp.int32, (q, q), 1)
    tril = row >= col
    cum = _dot_split_lhs01(jnp.where(tril, 1.0, 0.0).astype(BF16), d_a, 3)
    cum_last = cum[q - 1:q, :]
    ecum = jnp.exp(cum)
    dec_end = jnp.exp(cum_last - cum)
    cdec = jnp.broadcast_to(jnp.exp(cum_last), (BF16_ROWS, LANES))

    rep = rep_ref[...]
    expanded = _dot_split(jnp.concatenate([dt, ecum, dec_end], axis=0), rep, 2)
    dt_e = expanded[:q]
    ecum_e = expanded[q:2 * q]
    dec_e = expanded[2 * q:3 * q]
    cdec_e = _dot_split(cdec, rep, 2)[:1]

    xdt = xs * dt_e
    xdt_b = xdt.astype(BF16)
    xdtd_b = (xdt * dec_e).astype(BF16)
    cum_t = _transpose_rows_to_lanes(cum)
    left = lax.broadcasted_iota(jnp.int32, (q, LANES), 1) < SSD_HEAD_DIM

    def masked_decay(cb, h):
        seg = cum[:, h:h + 1] - cum_t[h:h + 1, :]
        return (cb * jnp.exp(jnp.where(tril, seg, -jnp.inf))).astype(BF16)

    ys = []
    for g in range(SSD_GROUPS):
        b_f32 = xc[:, D_INNER + g * D_STATE:D_INNER + (g + 1) * D_STATE]
        b_g = b_f32.astype(BF16)
        c_g = xc[:, D_INNER + BC_DIM + g * D_STATE:D_INNER + BC_DIM + (g + 1) * D_STATE].astype(BF16)
        cb = lax.dot_general(c_g, b_g, (((1,), (1,)), ((), ())), preferred_element_type=F32)
        st_g = st_ref[g]
        y_off = _dot(c_g, st_g.astype(BF16)) * ecum_e[:, g * gw:(g + 1) * gw]
        for kk in range(pairs_per_group):
            pair = g * pairs_per_group + kk
            m0 = masked_decay(cb, 2 * pair)
            m1 = masked_decay(cb, 2 * pair + 1)
            xpair = xdt_b[:, pair * LANES:(pair + 1) * LANES]
            y_diag = jnp.where(left, _dot(m0, xpair), _dot(m1, xpair))
            ys.append(y_diag + y_off[:, kk * LANES:(kk + 1) * LANES])
        b_t = _transpose_rows_to_lanes(b_f32).astype(BF16)
        st_ref[g] = st_g * cdec_e[:, g * gw:(g + 1) * gw] + _dot(b_t, xdtd_b[:, g * gw:(g + 1) * gw])

    y = jnp.concatenate(ys, axis=1) + xs * drep_ref[...]

    yg = y * jax.nn.silu(z_ref[...].astype(F32))
    normed = []
    for g in range(SSD_GROUPS):
        blk = yg[:, g * gw:(g + 1) * gw]
        normed.append(blk * lax.rsqrt(jnp.mean(blk * blk, axis=-1, keepdims=True) + EPS))
    ya_ref[...] = (jnp.concatenate(normed, axis=1) * ng_ref[...]).astype(BF16)

    @pl.when(c == nc - 1)
    def _():
        for k in range(SSD_HEADS // 2):
            g, kk = divmod(k, pairs_per_group)
            ssmo_ref[0, k] = st_ref[g, :, kk * LANES:(kk + 1) * LANES].T


def _dot_split_lhs01(lhs01_bf16, rhs_f32, n):
    parts = _split_bf16(rhs_f32, n)
    out = _dot(lhs01_bf16, parts[0])
    for p in parts[1:]:
        out = out + _dot(lhs01_bf16, p)
    return out


def _ssd(proj, dt_raw, conv0, ssm0, lw, nseq, seqlen):
    t = proj.shape[0]
    q = min(SSD_CHUNK, seqlen)
    nc = seqlen // q
    ssm0_pairs = ssm0.reshape(nseq, SSD_HEADS // 2, 2 * SSD_HEAD_DIM, D_STATE)
    conv0 = jnp.pad(conv0, ((0, 0), (CONV_PAD_ROWS - (CONV_W - 1), 0), (0, 0)))
    const = lambda b, c: (0, 0)
    rowblk = lambda b, c: (b * nc + c, 0)
    ya, conv_out, ssm_out = pl.pallas_call(
        functools.partial(_ssd_body, q=q, nc=nc),
        out_shape=(
            jax.ShapeDtypeStruct((t, D_INNER), BF16),
            jax.ShapeDtypeStruct((nseq, CONV_W - 1, XBC_DIM), F32),
            jax.ShapeDtypeStruct(ssm0_pairs.shape, F32),
        ),
        grid=(nseq, nc),
        in_specs=[
            pl.BlockSpec((q, XBC_DIM), rowblk),
            pl.BlockSpec((q, D_INNER), lambda b, c: (b * nc + c, Z_OFF // D_INNER)),
            pl.BlockSpec((q, LANES), rowblk),
            pl.BlockSpec((1, CONV_PAD_ROWS, XBC_DIM), lambda b, c: (b, 0, 0)),
            pl.BlockSpec((1,) + ssm0_pairs.shape[1:], lambda b, c: (b, 0, 0, 0)),
            pl.BlockSpec((CONV_W, XBC_DIM), const),
            pl.BlockSpec((1, XBC_DIM), const),
            pl.BlockSpec((1, LANES), const),
            pl.BlockSpec((1, LANES), const),
            pl.BlockSpec((1, D_INNER), const),
            pl.BlockSpec((1, D_INNER), const),
            pl.BlockSpec((LANES, D_INNER), const),
        ],
        out_specs=(
            pl.BlockSpec((q, D_INNER), rowblk),
            pl.BlockSpec((1, CONV_W - 1, XBC_DIM), lambda b, c: (b, 0, 0)),
            pl.BlockSpec((1,) + ssm0_pairs.shape[1:], lambda b, c: (b, 0, 0, 0)),
        ),
        scratch_shapes=[
            pltpu.VMEM((q, XBC_DIM), BF16),
            pltpu.VMEM((CONV_PAD_ROWS, XBC_DIM), F32),
            pltpu.VMEM((SSD_GROUPS, D_STATE, HEADS_PER_GROUP * SSD_HEAD_DIM), F32),
        ],
        compiler_params=_params(("parallel", "arbitrary"), 48),
        name="ssd",
    )(proj, proj, dt_raw, conv0, ssm0_pairs, lw["conv_w"], lw["conv_b"], lw["dt_bias"], lw["a_log"],
      lw["d_rep"], lw["ssd_norm"], lw["head_rep"])
    return ya, conv_out, ssm_out.reshape(ssm0.shape)


def _gmlp_body(uv_ref, lg_ref, lb_ref, ws_ref, bst_ref, yb_ref, *maybe_vn_ref, tc, nchunks):
    uv = jax.nn.gelu(uv_ref[...].astype(F32))
    u = uv[:, :GM_WIDTH]
    v = uv[:, GM_WIDTH:]
    mu = jnp.mean(v, axis=-1, keepdims=True)
    dev = v - mu
    var = jnp.mean(dev * dev, axis=-1, keepdims=True)
    vn = dev * lax.rsqrt(var + EPS) * lg_ref[...] + lb_ref[...]
    if maybe_vn_ref:
        maybe_vn_ref[0][...] = vn
    vb = vn.astype(BF16)

    row = lax.broadcasted_iota(jnp.int32, (tc, tc), 0)
    col = lax.broadcasted_iota(jnp.int32, (tc, tc), 1)
    tril = row >= col
    mixed = [[None] * GM_GROUPS for _ in range(nchunks)]
    for g in range(GM_GROUPS):
        w = jnp.where(tril, ws_ref[g, :tc, :tc], 0.0).astype(BF16)
        bias = bst_ref[:tc, g:g + 1]
        cols = slice(g * GM_GROUP_DIM, (g + 1) * GM_GROUP_DIM)
        if tc == LANES:
            rhs = jnp.concatenate([vb[r * tc:(r + 1) * tc, cols] for r in range(nchunks)], axis=1)
            out = _dot(w, rhs) + bias
            for r in range(nchunks):
                mixed[r][g] = out[:, r * GM_GROUP_DIM:(r + 1) * GM_GROUP_DIM]
        else:
            for r in range(nchunks):
                mixed[r][g] = _dot(w, vb[r * tc:(r + 1) * tc, cols]) + bias
    full = jnp.concatenate([jnp.concatenate(m, axis=1) for m in mixed], axis=0)
    yb_ref[...] = (u * full).astype(BF16)


def _gmlp(proj, lw, seqlen, keep_v):
    t = proj.shape[0]
    tc = min(GM_CHUNK, seqlen)
    rows = _row_tile(t, 512)
    nchunks = rows // tc
    const2 = lambda i: (0, 0)
    out_shape = [jax.ShapeDtypeStruct((t, GM_WIDTH), BF16)]
    out_specs = [pl.BlockSpec((rows, GM_WIDTH), lambda i: (i, 0))]
    if keep_v:
        out_shape.append(jax.ShapeDtypeStruct((t, GM_WIDTH), F32))
        out_specs.append(pl.BlockSpec((rows, GM_WIDTH), lambda i: (i, 0)))
    outs = pl.pallas_call(
        functools.partial(_gmlp_body, tc=tc, nchunks=nchunks),
        out_shape=tuple(out_shape),
        grid=(t // rows,),
        in_specs=[
            pl.BlockSpec((rows, 2 * GM_WIDTH), lambda i: (i, UV_OFF // (2 * GM_WIDTH))),
            pl.BlockSpec((1, GM_WIDTH), const2),
            pl.BlockSpec((1, GM_WIDTH), const2),
            pl.BlockSpec((GM_GROUPS, GM_CHUNK, GM_CHUNK), lambda i: (0, 0, 0)),
            pl.BlockSpec((GM_CHUNK, GM_GROUPS), const2),
        ],
        out_specs=tuple(out_specs),
        compiler_params=_params(("parallel",), 48),
        name="gmlp",
    )(proj, lw["gm_ln_g"], lw["gm_ln_b"], lw["gm_ws"], lw["gm_bs_t"])
    return (outs[0], outs[1]) if keep_v else (outs[0], None)


def _merge_body(ya_ref, yb_ref, ga_ref, gb_ref, x_ref, wa_ref, wb_ref, wo_ref, o_ref):
    merged = (jax.nn.sigmoid(ga_ref[...].astype(F32)) * _dot(ya_ref[...], wa_ref[...])
              + jax.nn.sigmoid(gb_ref[...].astype(F32)) * _dot(yb_ref[...], wb_ref[...]))
    o_ref[...] = x_ref[...] + _dot(merged.astype(BF16), wo_ref[...])


def _resident(block_shape, index_map):
    return pl.BlockSpec(block_shape, index_map, pipeline_mode=pl.Buffered(1))


def _merge(ya, yb, proj, x, wa, wb, wo, layer):
    t, d = x.shape
    tm = _row_tile(t, 256)
    ga = GATE_OFF // d
    whole = lambda i: (layer, 0, 0)
    return pl.pallas_call(
        _merge_body,
        out_shape=jax.ShapeDtypeStruct((t, d), F32),
        grid=(t // tm,),
        in_specs=[
            pl.BlockSpec((tm, D_INNER), lambda i: (i, 0)),
            pl.BlockSpec((tm, GM_WIDTH), lambda i: (i, 0)),
            pl.BlockSpec((tm, d), lambda i: (i, ga)),
            pl.BlockSpec((tm, d), lambda i: (i, ga + 1)),
            pl.BlockSpec((tm, d), lambda i: (i, 0)),
            _resident((None, D_INNER, d), whole),
            _resident((None, GM_WIDTH, d), whole),
            _resident((None, d, d), whole),
        ],
        out_specs=pl.BlockSpec((tm, d), lambda i: (i, 0)),
        compiler_params=_params(("parallel",), 52),
        name="merge",
    )(ya, yb, proj, proj, x, wa, wb, wo)


def _ple_body(x_ref, pe_ref, g_ref, wg_ref, wp_ref, *rest, final):
    o_ref = rest[-1]
    x = x_ref[...]
    h = _rmsnorm(x, g_ref[...]).astype(BF16)
    out = x + _dot(pe_ref[...].astype(BF16), wp_ref[...]) * jax.nn.sigmoid(_dot(h, wg_ref[...]))
    if final:
        out = _rmsnorm(out, rest[0][...])
    o_ref[...] = out


def _ple(x, pe, g, wg, wp, final_g, layer):
    t, d = x.shape
    tm = _row_tile(t, 512)
    final = final_g is not None
    const = lambda i: (0, 0)
    whole = lambda i: (layer, 0, 0)
    in_specs = [
        pl.BlockSpec((tm, d), lambda i: (i, 0)),
        pl.BlockSpec((tm, pe.shape[1]), lambda i: (i, 0)),
        pl.BlockSpec((1, d), const),
        _resident((None, d, d), whole),
        _resident((None, pe.shape[1], d), whole),
    ]
    args = [x, pe, g, wg, wp]
    if final:
        in_specs.append(pl.BlockSpec((1, d), const))
        args.append(final_g)
    return pl.pallas_call(
        functools.partial(_ple_body, final=final),
        out_shape=jax.ShapeDtypeStruct((t, d), F32),
        grid=(t // tm,),
        in_specs=in_specs,
        out_specs=pl.BlockSpec((tm, d), lambda i: (i, 0)),
        compiler_params=_params(("parallel",), 48),
        name="ple_final" if final else "ple",
    )(*args)


def _stacked_weights(w):
    w_in = w["w_in"]
    o1 = D_INNER
    o2 = o1 + XBC_DIM
    o3 = o2 + SSD_HEADS
    stacks = {k: w[k].astype(BF16) for k in (
        "ffn1_w_gate", "ffn1_w_up", "ffn1_w_down", "ffn2_w_gate", "ffn2_w_up", "ffn2_w_down",
        "w_branch_ssd", "w_branch_gmlp", "w_out", "ple_w_gate", "ple_w_proj")}
    stacks["w_in"] = jnp.concatenate([w_in[..., o1:o2], w_in[..., :o1], w_in[..., o3:]], axis=-1).astype(BF16)
    stacks["w_dt"] = jnp.pad(w_in[..., o2:o3], ((0, 0), (0, 0), (0, LANES - SSD_HEADS))).astype(BF16)
    return stacks


def _layer_weights(i, w):
    pad_heads = lambda v: jnp.pad(v[i].reshape(1, SSD_HEADS), ((0, 0), (0, LANES - SSD_HEADS)))
    rowvec = lambda v: v[i].reshape(1, -1)
    head_of_channel = jnp.arange(D_INNER, dtype=jnp.int32) // SSD_HEAD_DIM
    head_rep = (jnp.arange(LANES, dtype=jnp.int32)[:, None] == head_of_channel[None, :]).astype(BF16)
    return {
        "ffn1_norm": rowvec(w["ffn1_norm"]),
        "mix_norm": rowvec(w["mix_norm"]),
        "conv_w": w["conv_w"][i],
        "conv_b": rowvec(w["conv_b"]),
        "dt_bias": pad_heads(w["dt_bias"]),
        "a_log": pad_heads(w["a_log"]),
        "d_rep": jnp.repeat(w["d_skip"][i], SSD_HEAD_DIM).reshape(1, D_INNER),
        "ssd_norm": rowvec(w["ssd_norm"]),
        "head_rep": head_rep,
        "gm_ln_g": rowvec(w["gm_ln_g"]),
        "gm_ln_b": rowvec(w["gm_ln_b"]),
        "gm_ws": w["gm_ws"][i],
        "gm_bs_t": w["gm_bs"][i].T,
        "ffn2_norm": rowvec(w["ffn2_norm"]),
        "ple_norm": rowvec(w["ple_norm"]),
    }


def _trunk(x, pe, conv_state, ssm_state, sw, layers, final_norm, keep_v):
    nseq, seqlen, d = x.shape
    depth = len(layers)
    t = nseq * seqlen
    x = x.reshape(t, d)
    new_conv, new_ssm, new_v = [], [], []
    for i, lw in enumerate(layers):
        x = _ffn(x, lw["ffn1_norm"], sw["ffn1_w_gate"], sw["ffn1_w_up"], sw["ffn1_w_down"], i)
        proj, dt_raw = _inproj(x, lw["mix_norm"], sw["w_in"], sw["w_dt"], i)
        ya, conv_out, ssm_out = _ssd(proj, dt_raw, conv_state[i], ssm_state[i], lw, nseq, seqlen)
        yb, vn = _gmlp(proj, lw, seqlen, keep_v)
        x = _merge(ya, yb, proj, x, sw["w_branch_ssd"], sw["w_branch_gmlp"], sw["w_out"], i)
        x = _ffn(x, lw["ffn2_norm"], sw["ffn2_w_gate"], sw["ffn2_w_up"], sw["ffn2_w_down"], i)
        x = _ple(x, pe[i].reshape(t, -1), lw["ple_norm"], sw["ple_w_gate"], sw["ple_w_proj"],
                 final_norm if i == depth - 1 else None, i)
        new_conv.append(conv_out)
        new_ssm.append(ssm_out)
        if keep_v:
            new_v.append(vn.reshape(nseq, seqlen, GM_WIDTH))
    y = x.reshape(nseq, seqlen, d)
    return y, jnp.stack(new_conv), jnp.stack(new_ssm), (jnp.stack(new_v) if keep_v else None)


def kernel(x_prompt, x_sample, p_prompt, p_sample, state_conv, state_ssm, ffn1_norm, ffn1_w_gate, ffn1_w_up, ffn1_w_down, mix_norm, w_in, conv_w, conv_b, dt_bias, a_log, d_skip, ssd_norm, gm_ln_g, gm_ln_b, gm_ws, gm_bs, w_branch_ssd, w_branch_gmlp, w_out, ffn2_norm, ffn2_w_gate, ffn2_w_up, ffn2_w_down, ple_norm, ple_w_gate, ple_w_proj, final_norm):
    w = {
        "ffn1_norm": ffn1_norm, "ffn1_w_gate": ffn1_w_gate, "ffn1_w_up": ffn1_w_up, "ffn1_w_down": ffn1_w_down,
        "mix_norm": mix_norm, "w_in": w_in, "conv_w": conv_w, "conv_b": conv_b,
        "dt_bias": dt_bias, "a_log": a_log, "d_skip": d_skip, "ssd_norm": ssd_norm,
        "gm_ln_g": gm_ln_g, "gm_ln_b": gm_ln_b, "gm_ws": gm_ws, "gm_bs": gm_bs,
        "w_branch_ssd": w_branch_ssd, "w_branch_gmlp": w_branch_gmlp, "w_out": w_out,
        "ffn2_norm": ffn2_norm, "ffn2_w_gate": ffn2_w_gate, "ffn2_w_up": ffn2_w_up, "ffn2_w_down": ffn2_w_down,
        "ple_norm": ple_norm, "ple_w_gate": ple_w_gate, "ple_w_proj": ple_w_proj,
    }
    depth = w_in.shape[0]
    layers = [_layer_weights(i, w) for i in range(depth)]
    sw = _stacked_weights(w)
    fnorm = final_norm.reshape(1, -1)
    bp = x_prompt.shape[0]
    conv0 = jnp.zeros((depth, bp, CONV_W - 1, XBC_DIM), F32)
    ssm0 = jnp.zeros((depth, bp, SSD_HEADS, SSD_HEAD_DIM, D_STATE), F32)
    y_prompt, conv_prompt, ssm_prompt, _ = _trunk(x_prompt, p_prompt, conv0, ssm0, sw, layers, fnorm, False)
    y_sample, conv_sample, ssm_sample, v_sample = _trunk(x_sample, p_sample, state_conv, state_ssm, sw, layers, fnorm, True)
    return (y_prompt, y_sample, ssm_prompt, conv_prompt, ssm_sample, conv_sample, v_sample)
```

```python
import functools

import jax
import jax.numpy as jnp
from jax import lax
from jax.experimental import pallas as pl
from jax.experimental.pallas import tpu as pltpu

F32 = jnp.float32
BF16 = jnp.bfloat16

D_MODEL = 2048
SSD_HEADS = 32
SSD_HEAD_DIM = 64
D_INNER = SSD_HEADS * SSD_HEAD_DIM
SSD_GROUPS = 4
HEADS_PER_GROUP = SSD_HEADS // SSD_GROUPS
D_STATE = 128
CONV_W = 4
BC_DIM = SSD_GROUPS * D_STATE
XBC_DIM = D_INNER + 2 * BC_DIM
GM_CHUNK = 128
GM_GROUPS = 8
GM_GROUP_DIM = 128
GM_WIDTH = GM_GROUPS * GM_GROUP_DIM
EPS = 1e-6

LANES = 128
SUBLANES = 8
MIB = 1024 * 1024

PROJ_TN = 1024
XBC_OFF = 0
Z_OFF = 4096
UV_OFF = 6144
GATE_OFF = 8192
PROJ_COLS = 12288
PROJ_PAD_TILE = XBC_DIM // PROJ_TN
PROJ_W_COLS = XBC_DIM + D_INNER + 2 * GM_WIDTH + 2 * D_MODEL

SSD_CHUNK = 128
BF16_ROWS = 2 * SUBLANES
CONV_PAD_ROWS = BF16_ROWS


def _params(semantics, vmem_mib):
    return pltpu.CompilerParams(dimension_semantics=semantics, vmem_limit_bytes=vmem_mib * MIB)


def _rmsnorm(x, g):
    return x * lax.rsqrt(jnp.mean(x * x, axis=-1, keepdims=True) + EPS) * g


def _dot(a, b):
    return jnp.dot(a, b, preferred_element_type=F32)


def _row_tile(t, want):
    return want if t % want == 0 else t


def _ffn_body(x_ref, g_ref, wg_ref, wu_ref, wd_ref, o_ref, h_ref):
    j = pl.program_id(1)

    @pl.when(j == 0)
    def _():
        x = x_ref[...]
        h_ref[...] = _rmsnorm(x, g_ref[...]).astype(BF16)
        o_ref[...] = x

    h = h_ref[...]
    act = (0.5 * jax.nn.silu(_dot(h, wg_ref[...])) * _dot(h, wu_ref[...])).astype(BF16)
    o_ref[...] += _dot(act, wd_ref[...])


def _ffn(x, g, wg, wu, wd, layer):
    t, d = x.shape
    dff = wg.shape[-1]
    tm = _row_tile(t, 1024)
    tf = 512
    return pl.pallas_call(
        _ffn_body,
        out_shape=jax.ShapeDtypeStruct((t, d), F32),
        grid=(t // tm, dff // tf),
        in_specs=[
            pl.BlockSpec((tm, d), lambda i, j: (i, 0)),
            pl.BlockSpec((1, d), lambda i, j: (0, 0)),
            pl.BlockSpec((None, d, tf), lambda i, j: (layer, 0, j)),
            pl.BlockSpec((None, d, tf), lambda i, j: (layer, 0, j)),
            pl.BlockSpec((None, tf, d), lambda i, j: (layer, j, 0)),
        ],
        out_specs=pl.BlockSpec((tm, d), lambda i, j: (i, 0)),
        scratch_shapes=[pltpu.VMEM((tm, d), BF16)],
        compiler_params=_params(("parallel", "arbitrary"), 58),
        name="ffn",
    )(x, g, wg, wu, wd)


def _inproj_body(x_ref, g_ref, w_ref, wdt_ref, o_ref, dt_ref, h_ref):
    @pl.when(pl.program_id(1) == 0)
    def _():
        h = _rmsnorm(x_ref[...], g_ref[...]).astype(BF16)
        h_ref[...] = h
        dt_ref[...] = _dot(h, wdt_ref[...])

    o_ref[...] = _dot(h_ref[...], w_ref[...]).astype(o_ref.dtype)


def _inproj(x, g, w, wdt, layer):
    t, d = x.shape
    tm = _row_tile(t, 1024)
    n_tiles = PROJ_W_COLS // PROJ_TN
    return pl.pallas_call(
        _inproj_body,
        out_shape=(jax.ShapeDtypeStruct((t, PROJ_COLS), BF16), jax.ShapeDtypeStruct((t, LANES), F32)),
        grid=(t // tm, n_tiles),
        in_specs=[
            pl.BlockSpec((tm, d), lambda i, j: (i, 0)),
            pl.BlockSpec((1, d), lambda i, j: (0, 0)),
            pl.BlockSpec((None, d, PROJ_TN), lambda i, j: (layer, 0, j)),
            pl.BlockSpec((None, d, LANES), lambda i, j: (layer, 0, 0)),
        ],
        out_specs=(
            pl.BlockSpec((tm, PROJ_TN), lambda i, j: (i, jnp.where(j >= PROJ_PAD_TILE, j + 1, j))),
            pl.BlockSpec((tm, LANES), lambda i, j: (i, 0)),
        ),
        scratch_shapes=[pltpu.VMEM((tm, d), BF16)],
        compiler_params=_params(("parallel", "arbitrary"), 52),
        name="inproj",
    )(x, g, w, wdt)


def _split_bf16(v, n):
    parts = []
    rest = v
    for _ in range(n):
        p = rest.astype(BF16)
        parts.append(p)
        rest = rest - p.astype(F32)
    return parts


def _dot_split(a_f32, b_bf16, n):
    parts = _split_bf16(a_f32, n)
    out = _dot(parts[0], b_bf16)
    for p in parts[1:]:
        out = out + _dot(p, b_bf16)
    return out


def _dot_split_lhs01(lhs01_bf16, rhs_f32, n):
    parts = _split_bf16(rhs_f32, n)
    out = _dot(lhs01_bf16, parts[0])
    for p in parts[1:]:
        out = out + _dot(lhs01_bf16, p)
    return out


def _transpose_rows_to_lanes(x):
    q = x.shape[0]
    if q == LANES:
        return x.T
    pad = jnp.zeros((LANES - q, x.shape[1]), x.dtype)
    return jnp.concatenate([x, pad], axis=0).T[:, :q]


def _conv_silu_chunk(cur, prev_ref, cw_ref, cb_ref, q):
    both = jnp.concatenate([prev_ref[...], cur], axis=0)
    srow = lax.broadcasted_iota(jnp.int32, (q, 2 * q), 0)
    scol = lax.broadcasted_iota(jnp.int32, (q, 2 * q), 1)
    shift_mats = [jnp.where(scol == srow + (q - (CONV_W - 1) + k), 1.0, 0.0).astype(BF16)
                  for k in range(CONV_W - 1)]
    shifted = _dot(jnp.concatenate(shift_mats, axis=0), both)
    acc = cb_ref[...]
    for k in range(CONV_W - 1):
        acc = acc + shifted[k * q:(k + 1) * q] * cw_ref[k:k + 1, :]
    acc = acc + cur.astype(F32) * cw_ref[CONV_W - 1:CONV_W, :]
    prev_ref[...] = cur
    return jax.nn.silu(acc)


def _ssd_body(xbc_ref, z_ref, dt_ref, conv0_ref, ssm0_ref, cw_ref, cb_ref, dtb_ref, alog_ref, drep_ref,
              ng_ref, rep_ref, ya_ref, convo_ref, ssmo_ref, prev_ref, tail_ref, st_ref, *, q, nc):
    c = pl.program_id(1)
    pairs_per_group = HEADS_PER_GROUP // 2
    gw = HEADS_PER_GROUP * SSD_HEAD_DIM

    @pl.when(c == 0)
    def _():
        prev_ref[...] = jnp.zeros_like(prev_ref)
        prev_ref[q - CONV_PAD_ROWS:q, :] = conv0_ref[0].astype(BF16)
        for k in range(SSD_HEADS // 2):
            g, kk = divmod(k, pairs_per_group)
            st_ref[g, :, kk * LANES:(kk + 1) * LANES] = ssm0_ref[0, k].T

    xc = _conv_silu_chunk(xbc_ref[...], prev_ref, cw_ref, cb_ref, q)
    xs = xc[:, :D_INNER]
    xs_b = xs.astype(BF16)

    dt = jax.nn.softplus(dt_ref[...] + dtb_ref[...])
    d_a = dt * (-jnp.exp(alog_ref[...]))
    row = lax.broadcasted_iota(jnp.int32, (q, q), 0)
    col = lax.broadcasted_iota(jnp.int32, (q, q), 1)
    tril = row >= col
    cum = _dot_split_lhs01(jnp.where(tril, 1.0, 0.0).astype(BF16), d_a, 3)
    cum_last = cum[q - 1:q, :]
    ecum = jnp.exp(cum)
    dec_end = jnp.exp(cum_last - cum)
    cdec = jnp.broadcast_to(jnp.exp(cum_last), (BF16_ROWS, LANES))

    rep = rep_ref[...]
    expanded = _dot_split(jnp.concatenate([ecum, dt * dec_end], axis=0), rep, 2)
    ecum_e = expanded[:q]
    dtdec_e = expanded[q:2 * q]
    cdec_e = _dot_split(cdec, rep, 2)[:1]

    xdtd_b = (xs * dtdec_e).astype(BF16)
    cum_t = _transpose_rows_to_lanes(cum)
    dt_t = _transpose_rows_to_lanes(dt)
    left = lax.broadcasted_iota(jnp.int32, (q, LANES), 1) < SSD_HEAD_DIM

    def masked_decay(cb, h):
        seg = cum[:, h:h + 1] - cum_t[h:h + 1, :]
        return (cb * jnp.exp(jnp.where(tril, seg, -jnp.inf)) * dt_t[h:h + 1, :]).astype(BF16)

    ys = []
    for g in range(SSD_GROUPS):
        b_f32 = xc[:, D_INNER + g * D_STATE:D_INNER + (g + 1) * D_STATE]
        b_g = b_f32.astype(BF16)
        c_g = xc[:, D_INNER + BC_DIM + g * D_STATE:D_INNER + BC_DIM + (g + 1) * D_STATE].astype(BF16)
        cb = lax.dot_general(c_g, b_g, (((1,), (1,)), ((), ())), preferred_element_type=F32)
        st_g = st_ref[g]
        y_off = _dot(c_g, st_g.astype(BF16)) * ecum_e[:, g * gw:(g + 1) * gw]
        for kk in range(pairs_per_group):
            pair = g * pairs_per_group + kk
            m0 = masked_decay(cb, 2 * pair)
            m1 = masked_decay(cb, 2 * pair + 1)
            xpair = xs_b[:, pair * LANES:(pair + 1) * LANES]
            if q % LANES == 0:
                x_l = jnp.where(left, xpair, jnp.zeros_like(xpair))
                x_r = jnp.where(left, jnp.zeros_like(xpair), xpair)
                y_diag = _dot(jnp.concatenate([m0, m1], axis=1), jnp.concatenate([x_l, x_r], axis=0))
            else:
                y_diag = jnp.where(left, _dot(m0, xpair), _dot(m1, xpair))
            ys.append(y_diag + y_off[:, kk * LANES:(kk + 1) * LANES])
        b_t = _transpose_rows_to_lanes(b_f32).astype(BF16)
        st_ref[g] = st_g * cdec_e[:, g * gw:(g + 1) * gw] + _dot(b_t, xdtd_b[:, g * gw:(g + 1) * gw])

    y = jnp.concatenate(ys, axis=1) + xs * drep_ref[...]

    yg = y * jax.nn.silu(z_ref[...].astype(F32))
    normed = []
    for g in range(SSD_GROUPS):
        blk = yg[:, g * gw:(g + 1) * gw]
        normed.append(blk * lax.rsqrt(jnp.mean(blk * blk, axis=-1, keepdims=True) + EPS))
    ya_ref[...] = (jnp.concatenate(normed, axis=1) * ng_ref[...]).astype(BF16)

    @pl.when(c == nc - 1)
    def _():
        tail_ref[...] = xbc_ref[q - CONV_PAD_ROWS:q, :].astype(F32)
        convo_ref[0] = tail_ref[CONV_PAD_ROWS - (CONV_W - 1):CONV_PAD_ROWS, :]
        for k in range(SSD_HEADS // 2):
            g, kk = divmod(k, pairs_per_group)
            ssmo_ref[0, k] = st_ref[g, :, kk * LANES:(kk + 1) * LANES].T


def _ssd(proj, dt_raw, conv0, ssm0, lw, nseq, seqlen):
    t = proj.shape[0]
    q = min(SSD_CHUNK, seqlen)
    nc = seqlen // q
    ssm0_pairs = ssm0.reshape(nseq, SSD_HEADS // 2, 2 * SSD_HEAD_DIM, D_STATE)
    conv0 = jnp.pad(conv0, ((0, 0), (CONV_PAD_ROWS - (CONV_W - 1), 0), (0, 0)))
    const = lambda b, c: (0, 0)
    rowblk = lambda b, c: (b * nc + c, 0)
    ya, conv_out, ssm_out = pl.pallas_call(
        functools.partial(_ssd_body, q=q, nc=nc),
        out_shape=(
            jax.ShapeDtypeStruct((t, D_INNER), BF16),
            jax.ShapeDtypeStruct((nseq, CONV_W - 1, XBC_DIM), F32),
            jax.ShapeDtypeStruct(ssm0_pairs.shape, F32),
        ),
        grid=(nseq, nc),
        in_specs=[
            pl.BlockSpec((q, XBC_DIM), rowblk),
            pl.BlockSpec((q, D_INNER), lambda b, c: (b * nc + c, Z_OFF // D_INNER)),
            pl.BlockSpec((q, LANES), rowblk),
            pl.BlockSpec((1, CONV_PAD_ROWS, XBC_DIM), lambda b, c: (b, 0, 0)),
            pl.BlockSpec((1,) + ssm0_pairs.shape[1:], lambda b, c: (b, 0, 0, 0)),
            pl.BlockSpec((CONV_W, XBC_DIM), const),
            pl.BlockSpec((1, XBC_DIM), const),
            pl.BlockSpec((1, LANES), const),
            pl.BlockSpec((1, LANES), const),
            pl.BlockSpec((1, D_INNER), const),
            pl.BlockSpec((1, D_INNER), const),
            pl.BlockSpec((LANES, D_INNER), const),
        ],
        out_specs=(
            pl.BlockSpec((q, D_INNER), rowblk),
            pl.BlockSpec((1, CONV_W - 1, XBC_DIM), lambda b, c: (b, 0, 0)),
            pl.BlockSpec((1,) + ssm0_pairs.shape[1:], lambda b, c: (b, 0, 0, 0)),
        ),
        scratch_shapes=[
            pltpu.VMEM((q, XBC_DIM), BF16),
            pltpu.VMEM((CONV_PAD_ROWS, XBC_DIM), F32),
            pltpu.VMEM((SSD_GROUPS, D_STATE, HEADS_PER_GROUP * SSD_HEAD_DIM), F32),
        ],
        compiler_params=_params(("parallel", "arbitrary"), 48),
        name="ssd",
    )(proj, proj, dt_raw, conv0, ssm0_pairs, lw["conv_w"], lw["conv_b"], lw["dt_bias"], lw["a_log"],
      lw["d_rep"], lw["ssd_norm"], lw["head_rep"])
    return ya, conv_out, ssm_out.reshape(ssm0.shape)


def _gmlp_rows(uv_ref, lg_ref, lb_ref, ws_ref, bst_ref, vn_ref, tc, nchunks):
    uv = jax.nn.gelu(uv_ref[...].astype(F32))
    u = uv[:, :GM_WIDTH]
    v = uv[:, GM_WIDTH:]
    mu = jnp.mean(v, axis=-1, keepdims=True)
    dev = v - mu
    var = jnp.mean(dev * dev, axis=-1, keepdims=True)
    vn = dev * lax.rsqrt(var + EPS) * lg_ref[...] + lb_ref[...]
    if vn_ref is not None:
        vn_ref[...] = vn
    vb = vn.astype(BF16)

    row = lax.broadcasted_iota(jnp.int32, (tc, tc), 0)
    col = lax.broadcasted_iota(jnp.int32, (tc, tc), 1)
    tril = row >= col
    mixed = [[None] * GM_GROUPS for _ in range(nchunks)]
    for g in range(GM_GROUPS):
        w = jnp.where(tril, ws_ref[g, :tc, :tc], 0.0).astype(BF16)
        bias = bst_ref[:tc, g:g + 1]
        cols = slice(g * GM_GROUP_DIM, (g + 1) * GM_GROUP_DIM)
        if tc == LANES:
            rhs = jnp.concatenate([vb[r * tc:(r + 1) * tc, cols] for r in range(nchunks)], axis=1)
            out = _dot(w, rhs) + bias
            for r in range(nchunks):
                mixed[r][g] = out[:, r * GM_GROUP_DIM:(r + 1) * GM_GROUP_DIM]
        else:
            for r in range(nchunks):
                mixed[r][g] = _dot(w, vb[r * tc:(r + 1) * tc, cols]) + bias
    full = jnp.concatenate([jnp.concatenate(m, axis=1) for m in mixed], axis=0)
    return (u * full).astype(BF16)


def _merge_body(ya_ref, uv_ref, ga_ref, gb_ref, x_ref, lg_ref, lb_ref, ws_ref, bst_ref, wa_ref, wb_ref, wo_ref,
                o_ref, *maybe_vn_ref, tc, nchunks):
    pa = _dot(ya_ref[...], wa_ref[...])
    yb = _gmlp_rows(uv_ref, lg_ref, lb_ref, ws_ref, bst_ref, maybe_vn_ref[0] if maybe_vn_ref else None, tc, nchunks)
    merged = (jax.nn.sigmoid(ga_ref[...].astype(F32)) * pa
              + jax.nn.sigmoid(gb_ref[...].astype(F32)) * _dot(yb, wb_ref[...]))
    o_ref[...] = x_ref[...] + _dot(merged.astype(BF16), wo_ref[...])


def _resident(block_shape, index_map):
    return pl.BlockSpec(block_shape, index_map, pipeline_mode=pl.Buffered(1))


def _merge(ya, proj, x, lw, wa, wb, wo, layer, seqlen, keep_v):
    t, d = x.shape
    tm = _row_tile(t, 256)
    tc = min(GM_CHUNK, seqlen)
    ga = GATE_OFF // d
    whole = lambda i: (layer, 0, 0)
    const = lambda i: (0, 0)
    out_shape = [jax.ShapeDtypeStruct((t, d), F32)]
    out_specs = [pl.BlockSpec((tm, d), lambda i: (i, 0))]
    if keep_v:
        out_shape.append(jax.ShapeDtypeStruct((t, GM_WIDTH), F32))
        out_specs.append(pl.BlockSpec((tm, GM_WIDTH), lambda i: (i, 0)))
    outs = pl.pallas_call(
        functools.partial(_merge_body, tc=tc, nchunks=tm // tc),
        out_shape=tuple(out_shape),
        grid=(t // tm,),
        in_specs=[
            pl.BlockSpec((tm, D_INNER), lambda i: (i, 0)),
            pl.BlockSpec((tm, 2 * GM_WIDTH), lambda i: (i, UV_OFF // (2 * GM_WIDTH))),
            pl.BlockSpec((tm, d), lambda i: (i, ga)),
            pl.BlockSpec((tm, d), lambda i: (i, ga + 1)),
            pl.BlockSpec((tm, d), lambda i: (i, 0)),
            pl.BlockSpec((1, GM_WIDTH), const),
            pl.BlockSpec((1, GM_WIDTH), const),
            pl.BlockSpec((GM_GROUPS, GM_CHUNK, GM_CHUNK), lambda i: (0, 0, 0)),
            pl.BlockSpec((GM_CHUNK, GM_GROUPS), const),
            _resident((None, D_INNER, d), whole),
            _resident((None, GM_WIDTH, d), whole),
            _resident((None, d, d), whole),
        ],
        out_specs=tuple(out_specs),
        compiler_params=_params(("parallel",), 56),
        name="merge",
    )(ya, proj, proj, proj, x, lw["gm_ln_g"], lw["gm_ln_b"], lw["gm_ws"], lw["gm_bs_t"], wa, wb, wo)
    return (outs[0], outs[1]) if keep_v else (outs[0], None)


def _ple_body(x_ref, pe_ref, g_ref, wg_ref, wp_ref, *rest, final):
    o_ref = rest[-1]
    x = x_ref[...]
    h = _rmsnorm(x, g_ref[...]).astype(BF16)
    out = x + _dot(pe_ref[...].astype(BF16), wp_ref[...]) * jax.nn.sigmoid(_dot(h, wg_ref[...]))
    if final:
        out = _rmsnorm(out, rest[0][...])
    o_ref[...] = out


def _ple(x, pe, g, wg, wp, final_g, layer):
    t, d = x.shape
    tm = _row_tile(t, 512)
    final = final_g is not None
    const = lambda i: (0, 0)
    whole = lambda i: (layer, 0, 0)
    in_specs = [
        pl.BlockSpec((tm, d), lambda i: (i, 0)),
        pl.BlockSpec((tm, pe.shape[1]), lambda i: (i, 0)),
        pl.BlockSpec((1, d), const),
        _resident((None, d, d), whole),
        _resident((None, pe.shape[1], d), whole),
    ]
    args = [x, pe, g, wg, wp]
    if final:
        in_specs.append(pl.BlockSpec((1, d), const))
        args.append(final_g)
    return pl.pallas_call(
        functools.partial(_ple_body, final=final),
        out_shape=jax.ShapeDtypeStruct((t, d), F32),
        grid=(t // tm,),
        in_specs=in_specs,
        out_specs=pl.BlockSpec((tm, d), lambda i: (i, 0)),
        compiler_params=_params(("parallel",), 48),
        name="ple_final" if final else "ple",
    )(*args)


def _pack_w_in_body(w_ref, o_ref, dt_ref):
    o1 = D_INNER
    o2 = o1 + XBC_DIM
    o3 = o2 + SSD_HEADS
    o_ref[:, :XBC_DIM] = w_ref[:, o1:o2]
    o_ref[:, XBC_DIM:XBC_DIM + D_INNER] = w_ref[:, :o1]
    o_ref[:, XBC_DIM + D_INNER:] = w_ref[:, o3:]
    dt_cols = w_ref[:, o2:o2 + LANES]
    lane = lax.broadcasted_iota(jnp.int32, dt_cols.shape, 1)
    dt_ref[...] = jnp.where(lane < SSD_HEADS, dt_cols, jnp.zeros_like(dt_cols))


def _pack_w_in(w_in):
    depth, d, cols = w_in.shape
    rows = 256
    return pl.pallas_call(
        _pack_w_in_body,
        out_shape=(jax.ShapeDtypeStruct((depth, d, PROJ_W_COLS), BF16),
                   jax.ShapeDtypeStruct((depth, d, LANES), BF16)),
        grid=(depth, d // rows),
        in_specs=[pl.BlockSpec((None, rows, cols), lambda l, r: (l, r, 0))],
        out_specs=(pl.BlockSpec((None, rows, PROJ_W_COLS), lambda l, r: (l, r, 0)),
                   pl.BlockSpec((None, rows, LANES), lambda l, r: (l, r, 0))),
        compiler_params=_params(("parallel", "parallel"), 40),
        name="pack_w_in",
    )(w_in)


def _stacked_weights(w):
    stacks = {k: w[k].astype(BF16) for k in (
        "ffn1_w_gate", "ffn1_w_up", "ffn1_w_down", "ffn2_w_gate", "ffn2_w_up", "ffn2_w_down",
        "w_branch_ssd", "w_branch_gmlp", "w_out", "ple_w_gate", "ple_w_proj")}
    stacks["w_in"], stacks["w_dt"] = _pack_w_in(w["w_in"].astype(BF16))
    return stacks


def _layer_weights(i, w):
    pad_heads = lambda v: jnp.pad(v[i].reshape(1, SSD_HEADS), ((0, 0), (0, LANES - SSD_HEADS)))
    rowvec = lambda v: v[i].reshape(1, -1)
    head_of_channel = jnp.arange(D_INNER, dtype=jnp.int32) // SSD_HEAD_DIM
    head_rep = (jnp.arange(LANES, dtype=jnp.int32)[:, None] == head_of_channel[None, :]).astype(BF16)
    return {
        "ffn1_norm": rowvec(w["ffn1_norm"]),
        "mix_norm": rowvec(w["mix_norm"]),
        "conv_w": w["conv_w"][i],
        "conv_b": rowvec(w["conv_b"]),
        "dt_bias": pad_heads(w["dt_bias"]),
        "a_log": pad_heads(w["a_log"]),
        "d_rep": jnp.repeat(w["d_skip"][i], SSD_HEAD_DIM).reshape(1, D_INNER),
        "ssd_norm": rowvec(w["ssd_norm"]),
        "head_rep": head_rep,
        "gm_ln_g": rowvec(w["gm_ln_g"]),
        "gm_ln_b": rowvec(w["gm_ln_b"]),
        "gm_ws": w["gm_ws"][i],
        "gm_bs_t": w["gm_bs"][i].T,
        "ffn2_norm": rowvec(w["ffn2_norm"]),
        "ple_norm": rowvec(w["ple_norm"]),
    }


def _trunk(x, pe, conv_state, ssm_state, sw, layers, final_norm, keep_v):
    nseq, seqlen, d = x.shape
    depth = len(layers)
    t = nseq * seqlen
    x = x.reshape(t, d)
    new_conv, new_ssm, new_v = [], [], []
    for i, lw in enumerate(layers):
        x = _ffn(x, lw["ffn1_norm"], sw["ffn1_w_gate"], sw["ffn1_w_up"], sw["ffn1_w_down"], i)
        proj, dt_raw = _inproj(x, lw["mix_norm"], sw["w_in"], sw["w_dt"], i)
        ya, conv_out, ssm_out = _ssd(proj, dt_raw, conv_state[i], ssm_state[i], lw, nseq, seqlen)
        x, vn = _merge(ya, proj, x, lw, sw["w_branch_ssd"], sw["w_branch_gmlp"], sw["w_out"], i, seqlen, keep_v)
        x = _ffn(x, lw["ffn2_norm"], sw["ffn2_w_gate"], sw["ffn2_w_up"], sw["ffn2_w_down"], i)
        x = _ple(x, pe[i].reshape(t, -1), lw["ple_norm"], sw["ple_w_gate"], sw["ple_w_proj"],
                 final_norm if i == depth - 1 else None, i)
        new_conv.append(conv_out)
        new_ssm.append(ssm_out)
        if keep_v:
            new_v.append(vn.reshape(nseq, seqlen, GM_WIDTH))
    y = x.reshape(nseq, seqlen, d)
    return y, jnp.stack(new_conv), jnp.stack(new_ssm), (jnp.stack(new_v) if keep_v else None)


def kernel(x_prompt, x_sample, p_prompt, p_sample, state_conv, state_ssm, ffn1_norm, ffn1_w_gate, ffn1_w_up, ffn1_w_down, mix_norm, w_in, conv_w, conv_b, dt_bias, a_log, d_skip, ssd_norm, gm_ln_g, gm_ln_b, gm_ws, gm_bs, w_branch_ssd, w_branch_gmlp, w_out, ffn2_norm, ffn2_w_gate, ffn2_w_up, ffn2_w_down, ple_norm, ple_w_gate, ple_w_proj, final_norm):
    w = {
        "ffn1_norm": ffn1_norm, "ffn1_w_gate": ffn1_w_gate, "ffn1_w_up": ffn1_w_up, "ffn1_w_down": ffn1_w_down,
        "mix_norm": mix_norm, "w_in": w_in, "conv_w": conv_w, "conv_b": conv_b,
        "dt_bias": dt_bias, "a_log": a_log, "d_skip": d_skip, "ssd_norm": ssd_norm,
        "gm_ln_g": gm_ln_g, "gm_ln_b": gm_ln_b, "gm_ws": gm_ws, "gm_bs": gm_bs,
        "w_branch_ssd": w_branch_ssd, "w_branch_gmlp": w_branch_gmlp, "w_out": w_out,
        "ffn2_norm": ffn2_norm, "ffn2_w_gate": ffn2_w_gate, "ffn2_w_up": ffn2_w_up, "ffn2_w_down": ffn2_w_down,
        "ple_norm": ple_norm, "ple_w_gate": ple_w_gate, "ple_w_proj": ple_w_proj,
    }
    depth = w_in.shape[0]
    layers = [_layer_weights(i, w) for i in range(depth)]
    sw = _stacked_weights(w)
    fnorm = final_norm.reshape(1, -1)
    bp = x_prompt.shape[0]
    conv0 = jnp.zeros((depth, bp, CONV_W - 1, XBC_DIM), F32)
    ssm0 = jnp.zeros((depth, bp, SSD_HEADS, SSD_HEAD_DIM, D_STATE), F32)
    y_prompt, conv_prompt, ssm_prompt, _ = _trunk(x_prompt, p_prompt, conv0, ssm0, sw, layers, fnorm, False)
    y_sample, conv_sample, ssm_sample, v_sample = _trunk(x_sample, p_sample, state_conv, state_ssm, sw, layers, fnorm, True)
    return (y_prompt, y_sample, ssm_prompt, conv_prompt, ssm_sample, conv_sample, v_sample)
```

```python
import functools

import jax
import jax.numpy as jnp
from jax import lax
from jax.experimental import pallas as pl
from jax.experimental.pallas import tpu as pltpu

F32 = jnp.float32
BF16 = jnp.bfloat16

D_MODEL = 2048
SSD_HEADS = 32
SSD_HEAD_DIM = 64
D_INNER = SSD_HEADS * SSD_HEAD_DIM
SSD_GROUPS = 4
HEADS_PER_GROUP = SSD_HEADS // SSD_GROUPS
D_STATE = 128
CONV_W = 4
BC_DIM = SSD_GROUPS * D_STATE
XBC_DIM = D_INNER + 2 * BC_DIM
GM_CHUNK = 128
GM_GROUPS = 8
GM_GROUP_DIM = 128
GM_WIDTH = GM_GROUPS * GM_GROUP_DIM
EPS = 1e-6

LANES = 128
SUBLANES = 8
MIB = 1024 * 1024

PROJ_TN = 1024
XBC_OFF = 0
Z_OFF = 4096
UV_OFF = 6144
GATE_OFF = 8192
PROJ_COLS = 12288
PROJ_PAD_TILE = XBC_DIM // PROJ_TN
PROJ_W_COLS = XBC_DIM + D_INNER + 2 * GM_WIDTH + 2 * D_MODEL

SSD_CHUNK = 128
BF16_ROWS = 2 * SUBLANES
CONV_PAD_ROWS = BF16_ROWS


def _params(semantics, vmem_mib):
    return pltpu.CompilerParams(dimension_semantics=semantics, vmem_limit_bytes=vmem_mib * MIB)


def _rmsnorm(x, g):
    return x * lax.rsqrt(jnp.mean(x * x, axis=-1, keepdims=True) + EPS) * g


def _dot(a, b):
    return jnp.dot(a, b, preferred_element_type=F32)


def _row_tile(t, want):
    return want if t % want == 0 else t


def _ffn_body(x_ref, g_ref, wg_ref, wu_ref, wd_ref, o_ref, h_ref):
    j = pl.program_id(1)

    @pl.when(j == 0)
    def _():
        x = x_ref[...]
        h_ref[...] = _rmsnorm(x, g_ref[...]).astype(BF16)
        o_ref[...] = x

    h = h_ref[...]
    act = (0.5 * jax.nn.silu(_dot(h, wg_ref[...])) * _dot(h, wu_ref[...])).astype(BF16)
    o_ref[...] += _dot(act, wd_ref[...])


def _ffn(x, g, wg, wu, wd, layer):
    t, d = x.shape
    dff = wg.shape[-1]
    tm = _row_tile(t, 1024)
    tf = 512
    return pl.pallas_call(
        _ffn_body,
        out_shape=jax.ShapeDtypeStruct((t, d), F32),
        grid=(t // tm, dff // tf),
        in_specs=[
            pl.BlockSpec((tm, d), lambda i, j: (i, 0)),
            pl.BlockSpec((1, d), lambda i, j: (0, 0)),
            pl.BlockSpec((None, d, tf), lambda i, j: (layer, 0, j)),
            pl.BlockSpec((None, d, tf), lambda i, j: (layer, 0, j)),
            pl.BlockSpec((None, tf, d), lambda i, j: (layer, j, 0)),
        ],
        out_specs=pl.BlockSpec((tm, d), lambda i, j: (i, 0)),
        scratch_shapes=[pltpu.VMEM((tm, d), BF16)],
        compiler_params=_params(("parallel", "arbitrary"), 58),
        name="ffn",
    )(x, g, wg, wu, wd)


def _inproj_body(x_ref, g_ref, w_ref, wdt_ref, o_ref, dt_ref, h_ref):
    @pl.when(pl.program_id(1) == 0)
    def _():
        h = _rmsnorm(x_ref[...], g_ref[...]).astype(BF16)
        h_ref[...] = h
        dt_ref[...] = _dot(h, wdt_ref[...])

    o_ref[...] = _dot(h_ref[...], w_ref[...]).astype(o_ref.dtype)


def _inproj(x, g, w, wdt, layer):
    t, d = x.shape
    tm = _row_tile(t, 1024)
    n_tiles = PROJ_W_COLS // PROJ_TN
    return pl.pallas_call(
        _inproj_body,
        out_shape=(jax.ShapeDtypeStruct((t, PROJ_COLS), BF16), jax.ShapeDtypeStruct((t, LANES), F32)),
        grid=(t // tm, n_tiles),
        in_specs=[
            pl.BlockSpec((tm, d), lambda i, j: (i, 0)),
            pl.BlockSpec((1, d), lambda i, j: (0, 0)),
            pl.BlockSpec((None, d, PROJ_TN), lambda i, j: (layer, 0, j)),
            pl.BlockSpec((None, d, LANES), lambda i, j: (layer, 0, 0)),
        ],
        out_specs=(
            pl.BlockSpec((tm, PROJ_TN), lambda i, j: (i, jnp.where(j >= PROJ_PAD_TILE, j + 1, j))),
            pl.BlockSpec((tm, LANES), lambda i, j: (i, 0)),
        ),
        scratch_shapes=[pltpu.VMEM((tm, d), BF16)],
        compiler_params=_params(("parallel", "arbitrary"), 52),
        name="inproj",
    )(x, g, w, wdt)


def _split_bf16(v, n):
    parts = []
    rest = v
    for _ in range(n):
        p = rest.astype(BF16)
        parts.append(p)
        rest = rest - p.astype(F32)
    return parts


def _dot_split(a_f32, b_bf16, n):
    parts = _split_bf16(a_f32, n)
    out = _dot(parts[0], b_bf16)
    for p in parts[1:]:
        out = out + _dot(p, b_bf16)
    return out


def _dot_split_lhs01(lhs01_bf16, rhs_f32, n):
    parts = _split_bf16(rhs_f32, n)
    out = _dot(lhs01_bf16, parts[0])
    for p in parts[1:]:
        out = out + _dot(lhs01_bf16, p)
    return out


def _transpose_rows_to_lanes(x):
    q = x.shape[0]
    if q == LANES:
        return x.T
    pad = jnp.zeros((LANES - q, x.shape[1]), x.dtype)
    return jnp.concatenate([x, pad], axis=0).T[:, :q]


def _shift_matrices(q):
    t = jnp.arange(q, dtype=jnp.int32)[:, None]
    j = jnp.arange(2 * q, dtype=jnp.int32)[None, :]
    return jnp.concatenate([(j == t + (q - (CONV_W - 1) + k)) for k in range(CONV_W - 1)], axis=0).astype(BF16)


def _conv_silu_chunk(cur, prev_ref, shift_ref, cw_ref, cb_ref, q):
    both = jnp.concatenate([prev_ref[...], cur], axis=0)
    shifted = _dot(shift_ref[...], both)
    acc = cb_ref[...]
    for k in range(CONV_W - 1):
        acc = acc + shifted[k * q:(k + 1) * q] * cw_ref[k:k + 1, :]
    acc = acc + cur.astype(F32) * cw_ref[CONV_W - 1:CONV_W, :]
    prev_ref[...] = cur
    return jax.nn.silu(acc)


def _ssd_body(xbc_ref, z_ref, dt_ref, conv0_ref, ssm0_ref, cw_ref, cb_ref, dtb_ref, alog_ref, drep_ref,
              ng_ref, rep_ref, shift_ref, tri_ref, ya_ref, convo_ref, ssmo_ref, prev_ref, tail_ref, st_ref,
              *, q, nc):
    c = pl.program_id(1)
    pairs_per_group = HEADS_PER_GROUP // 2
    gw = HEADS_PER_GROUP * SSD_HEAD_DIM

    @pl.when(c == 0)
    def _():
        prev_ref[...] = jnp.zeros_like(prev_ref)
        prev_ref[q - CONV_PAD_ROWS:q, :] = conv0_ref[0].astype(BF16)
        for k in range(SSD_HEADS // 2):
            g, kk = divmod(k, pairs_per_group)
            st_ref[g, :, kk * LANES:(kk + 1) * LANES] = ssm0_ref[0, k].T

    xc = _conv_silu_chunk(xbc_ref[...], prev_ref, shift_ref, cw_ref, cb_ref, q)
    xs = xc[:, :D_INNER]
    xs_b = xs.astype(BF16)

    dt = jax.nn.softplus(dt_ref[...] + dtb_ref[...])
    d_a = dt * (-jnp.exp(alog_ref[...]))
    row = lax.broadcasted_iota(jnp.int32, (q, q), 0)
    col = lax.broadcasted_iota(jnp.int32, (q, q), 1)
    tril = row >= col
    cum = _dot_split_lhs01(tri_ref[...], d_a, 3)
    cum_last = cum[q - 1:q, :]
    ecum = jnp.exp(cum)
    dec_end = jnp.exp(cum_last - cum)
    cdec = jnp.broadcast_to(jnp.exp(cum_last), (BF16_ROWS, LANES))

    rep = rep_ref[...]
    expanded = _dot_split(jnp.concatenate([ecum, dt * dec_end], axis=0), rep, 2)
    ecum_e = expanded[:q]
    dtdec_e = expanded[q:2 * q]
    cdec_e = _dot_split(cdec, rep, 2)[:1]

    xdtd_b = (xs * dtdec_e).astype(BF16)
    src_t = _transpose_rows_to_lanes(cum - jnp.log(dt))
    left = lax.broadcasted_iota(jnp.int32, (q, LANES), 1) < SSD_HEAD_DIM

    def masked_decay(cb, h):
        seg = cum[:, h:h + 1] - src_t[h:h + 1, :]
        return (cb * jnp.exp(jnp.where(tril, seg, -jnp.inf))).astype(BF16)

    ys = []
    for g in range(SSD_GROUPS):
        b_f32 = xc[:, D_INNER + g * D_STATE:D_INNER + (g + 1) * D_STATE]
        b_g = b_f32.astype(BF16)
        c_g = xc[:, D_INNER + BC_DIM + g * D_STATE:D_INNER + BC_DIM + (g + 1) * D_STATE].astype(BF16)
        cb = lax.dot_general(c_g, b_g, (((1,), (1,)), ((), ())), preferred_element_type=F32)
        st_g = st_ref[g]
        y_off = _dot(c_g, st_g.astype(BF16)) * ecum_e[:, g * gw:(g + 1) * gw]
        for kk in range(pairs_per_group):
            pair = g * pairs_per_group + kk
            m0 = masked_decay(cb, 2 * pair)
            m1 = masked_decay(cb, 2 * pair + 1)
            xpair = xs_b[:, pair * LANES:(pair + 1) * LANES]
            if q % LANES == 0:
                x_l = jnp.where(left, xpair, jnp.zeros_like(xpair))
                x_r = jnp.where(left, jnp.zeros_like(xpair), xpair)
                y_diag = _dot(jnp.concatenate([m0, m1], axis=1), jnp.concatenate([x_l, x_r], axis=0))
            else:
                y_diag = jnp.where(left, _dot(m0, xpair), _dot(m1, xpair))
            ys.append(y_diag + y_off[:, kk * LANES:(kk + 1) * LANES])
        b_t = _transpose_rows_to_lanes(b_f32).astype(BF16)
        st_ref[g] = st_g * cdec_e[:, g * gw:(g + 1) * gw] + _dot(b_t, xdtd_b[:, g * gw:(g + 1) * gw])

    y = jnp.concatenate(ys, axis=1) + xs * drep_ref[...]

    yg = y * jax.nn.silu(z_ref[...].astype(F32))
    normed = []
    for g in range(SSD_GROUPS):
        blk = yg[:, g * gw:(g + 1) * gw]
        normed.append(blk * lax.rsqrt(jnp.mean(blk * blk, axis=-1, keepdims=True) + EPS))
    ya_ref[...] = (jnp.concatenate(normed, axis=1) * ng_ref[...]).astype(BF16)

    @pl.when(c == nc - 1)
    def _():
        tail_ref[...] = xbc_ref[q - CONV_PAD_ROWS:q, :].astype(F32)
        convo_ref[0] = tail_ref[CONV_PAD_ROWS - (CONV_W - 1):CONV_PAD_ROWS, :]
        for k in range(SSD_HEADS // 2):
            g, kk = divmod(k, pairs_per_group)
            ssmo_ref[0, k] = st_ref[g, :, kk * LANES:(kk + 1) * LANES].T


def _ssd(proj, dt_raw, conv0, ssm0, lw, nseq, seqlen):
    t = proj.shape[0]
    q = min(SSD_CHUNK, seqlen)
    nc = seqlen // q
    ssm0_pairs = ssm0.reshape(nseq, SSD_HEADS // 2, 2 * SSD_HEAD_DIM, D_STATE)
    conv0 = jnp.pad(conv0, ((0, 0), (CONV_PAD_ROWS - (CONV_W - 1), 0), (0, 0)))
    const = lambda b, c: (0, 0)
    rowblk = lambda b, c: (b * nc + c, 0)
    ya, conv_out, ssm_out = pl.pallas_call(
        functools.partial(_ssd_body, q=q, nc=nc),
        out_shape=(
            jax.ShapeDtypeStruct((t, D_INNER), BF16),
            jax.ShapeDtypeStruct((nseq, CONV_W - 1, XBC_DIM), F32),
            jax.ShapeDtypeStruct(ssm0_pairs.shape, F32),
        ),
        grid=(nseq, nc),
        in_specs=[
            pl.BlockSpec((q, XBC_DIM), rowblk),
            pl.BlockSpec((q, D_INNER), lambda b, c: (b * nc + c, Z_OFF // D_INNER)),
            pl.BlockSpec((q, LANES), rowblk),
            pl.BlockSpec((1, CONV_PAD_ROWS, XBC_DIM), lambda b, c: (b, 0, 0)),
            pl.BlockSpec((1,) + ssm0_pairs.shape[1:], lambda b, c: (b, 0, 0, 0)),
            pl.BlockSpec((CONV_W, XBC_DIM), const),
            pl.BlockSpec((1, XBC_DIM), const),
            pl.BlockSpec((1, LANES), const),
            pl.BlockSpec((1, LANES), const),
            pl.BlockSpec((1, D_INNER), const),
            pl.BlockSpec((1, D_INNER), const),
            pl.BlockSpec((LANES, D_INNER), const),
            pl.BlockSpec(((CONV_W - 1) * q, 2 * q), const),
            pl.BlockSpec((q, q), const),
        ],
        out_specs=(
            pl.BlockSpec((q, D_INNER), rowblk),
            pl.BlockSpec((1, CONV_W - 1, XBC_DIM), lambda b, c: (b, 0, 0)),
            pl.BlockSpec((1,) + ssm0_pairs.shape[1:], lambda b, c: (b, 0, 0, 0)),
        ),
        scratch_shapes=[
            pltpu.VMEM((q, XBC_DIM), BF16),
            pltpu.VMEM((CONV_PAD_ROWS, XBC_DIM), F32),
            pltpu.VMEM((SSD_GROUPS, D_STATE, HEADS_PER_GROUP * SSD_HEAD_DIM), F32),
        ],
        compiler_params=_params(("parallel", "arbitrary"), 48),
        name="ssd",
    )(proj, proj, dt_raw, conv0, ssm0_pairs, lw["conv_w"], lw["conv_b"], lw["dt_bias"], lw["a_log"],
      lw["d_rep"], lw["ssd_norm"], lw["head_rep"], _shift_matrices(q), jnp.tri(q, dtype=BF16))
    return ya, conv_out, ssm_out.reshape(ssm0.shape)


def _gmlp_rows(uv_ref, lg_ref, lb_ref, ws_ref, bst_ref, vn_ref, tc, nchunks):
    uv = jax.nn.gelu(uv_ref[...].astype(F32))
    u = uv[:, :GM_WIDTH]
    v = uv[:, GM_WIDTH:]
    mu = jnp.mean(v, axis=-1, keepdims=True)
    dev = v - mu
    var = jnp.mean(dev * dev, axis=-1, keepdims=True)
    vn = dev * lax.rsqrt(var + EPS) * lg_ref[...] + lb_ref[...]
    if vn_ref is not None:
        vn_ref[...] = vn
    vb = vn.astype(BF16)

    row = lax.broadcasted_iota(jnp.int32, (tc, tc), 0)
    col = lax.broadcasted_iota(jnp.int32, (tc, tc), 1)
    tril = row >= col
    mixed = [[None] * GM_GROUPS for _ in range(nchunks)]
    for g in range(GM_GROUPS):
        w = jnp.where(tril, ws_ref[g, :tc, :tc], 0.0).astype(BF16)
        bias = bst_ref[:tc, g:g + 1]
        cols = slice(g * GM_GROUP_DIM, (g + 1) * GM_GROUP_DIM)
        if tc == LANES:
            rhs = jnp.concatenate([vb[r * tc:(r + 1) * tc, cols] for r in range(nchunks)], axis=1)
            out = _dot(w, rhs) + bias
            for r in range(nchunks):
                mixed[r][g] = out[:, r * GM_GROUP_DIM:(r + 1) * GM_GROUP_DIM]
        else:
            for r in range(nchunks):
                mixed[r][g] = _dot(w, vb[r * tc:(r + 1) * tc, cols]) + bias
    full = jnp.concatenate([jnp.concatenate(m, axis=1) for m in mixed], axis=0)
    return (u * full).astype(BF16)


def _merge_body(ya_ref, uv_ref, ga_ref, gb_ref, x_ref, lg_ref, lb_ref, ws_ref, bst_ref, wa_ref, wb_ref, wo_ref,
                o_ref, *maybe_vn_ref, tc, nchunks):
    pa = _dot(ya_ref[...], wa_ref[...])
    yb = _gmlp_rows(uv_ref, lg_ref, lb_ref, ws_ref, bst_ref, maybe_vn_ref[0] if maybe_vn_ref else None, tc, nchunks)
    merged = (jax.nn.sigmoid(ga_ref[...].astype(F32)) * pa
              + jax.nn.sigmoid(gb_ref[...].astype(F32)) * _dot(yb, wb_ref[...]))
    o_ref[...] = x_ref[...] + _dot(merged.astype(BF16), wo_ref[...])


def _resident(block_shape, index_map):
    return pl.BlockSpec(block_shape, index_map, pipeline_mode=pl.Buffered(1))


def _merge(ya, proj, x, lw, wa, wb, wo, layer, seqlen, keep_v):
    t, d = x.shape
    tm = _row_tile(t, 256)
    tc = min(GM_CHUNK, seqlen)
    ga = GATE_OFF // d
    whole = lambda i: (layer, 0, 0)
    const = lambda i: (0, 0)
    out_shape = [jax.ShapeDtypeStruct((t, d), F32)]
    out_specs = [pl.BlockSpec((tm, d), lambda i: (i, 0))]
    if keep_v:
        out_shape.append(jax.ShapeDtypeStruct((t, GM_WIDTH), F32))
        out_specs.append(pl.BlockSpec((tm, GM_WIDTH), lambda i: (i, 0)))
    outs = pl.pallas_call(
        functools.partial(_merge_body, tc=tc, nchunks=tm // tc),
        out_shape=tuple(out_shape),
        grid=(t // tm,),
        in_specs=[
            pl.BlockSpec((tm, D_INNER), lambda i: (i, 0)),
            pl.BlockSpec((tm, 2 * GM_WIDTH), lambda i: (i, UV_OFF // (2 * GM_WIDTH))),
            pl.BlockSpec((tm, d), lambda i: (i, ga)),
            pl.BlockSpec((tm, d), lambda i: (i, ga + 1)),
            pl.BlockSpec((tm, d), lambda i: (i, 0)),
            pl.BlockSpec((1, GM_WIDTH), const),
            pl.BlockSpec((1, GM_WIDTH), const),
            pl.BlockSpec((GM_GROUPS, GM_CHUNK, GM_CHUNK), lambda i: (0, 0, 0)),
            pl.BlockSpec((GM_CHUNK, GM_GROUPS), const),
            _resident((None, D_INNER, d), whole),
            _resident((None, GM_WIDTH, d), whole),
            _resident((None, d, d), whole),
        ],
        out_specs=tuple(out_specs),
        compiler_params=_params(("parallel",), 56),
        name="merge",
    )(ya, proj, proj, proj, x, lw["gm_ln_g"], lw["gm_ln_b"], lw["gm_ws"], lw["gm_bs_t"], wa, wb, wo)
    return (outs[0], outs[1]) if keep_v else (outs[0], None)


def _ple_body(x_ref, pe_ref, g_ref, wg_ref, wp_ref, *rest, final):
    o_ref = rest[-1]
    x = x_ref[...]
    h = _rmsnorm(x, g_ref[...]).astype(BF16)
    out = x + _dot(pe_ref[...].astype(BF16), wp_ref[...]) * jax.nn.sigmoid(_dot(h, wg_ref[...]))
    if final:
        out = _rmsnorm(out, rest[0][...])
    o_ref[...] = out


def _ple(x, pe, g, wg, wp, final_g, layer):
    t, d = x.shape
    tm = _row_tile(t, 512)
    final = final_g is not None
    const = lambda i: (0, 0)
    whole = lambda i: (layer, 0, 0)
    in_specs = [
        pl.BlockSpec((tm, d), lambda i: (i, 0)),
        pl.BlockSpec((tm, pe.shape[1]), lambda i: (i, 0)),
        pl.BlockSpec((1, d), const),
        _resident((None, d, d), whole),
        _resident((None, pe.shape[1], d), whole),
    ]
    args = [x, pe, g, wg, wp]
    if final:
        in_specs.append(pl.BlockSpec((1, d), const))
        args.append(final_g)
    return pl.pallas_call(
        functools.partial(_ple_body, final=final),
        out_shape=jax.ShapeDtypeStruct((t, d), F32),
        grid=(t // tm,),
        in_specs=in_specs,
        out_specs=pl.BlockSpec((tm, d), lambda i: (i, 0)),
        compiler_params=_params(("parallel",), 48),
        name="ple_final" if final else "ple",
    )(*args)


def _pack_w_in_body(w_ref, tail_ref, o_ref, dt_ref):
    o1 = D_INNER
    o2 = o1 + XBC_DIM
    o3 = o2 + SSD_HEADS
    head_cols = w_ref.shape[-1]
    split = XBC_DIM + D_INNER + head_cols - o3
    o_ref[:, :XBC_DIM] = w_ref[:, o1:o2]
    o_ref[:, XBC_DIM:XBC_DIM + D_INNER] = w_ref[:, :o1]
    o_ref[:, XBC_DIM + D_INNER:split] = w_ref[:, o3:]
    o_ref[:, split:] = tail_ref[...]
    dt_cols = w_ref[:, o2:o2 + LANES]
    lane = lax.broadcasted_iota(jnp.int32, dt_cols.shape, 1)
    dt_ref[...] = jnp.where(lane < SSD_HEADS, dt_cols, jnp.zeros_like(dt_cols))


def _pack_w_in(w_in):
    depth, d, cols = w_in.shape
    rows = 256
    head_cols = cols // LANES * LANES
    head = w_in[..., :head_cols].astype(BF16)
    tail = w_in[..., head_cols:].astype(BF16)
    return pl.pallas_call(
        _pack_w_in_body,
        out_shape=(jax.ShapeDtypeStruct((depth, d, PROJ_W_COLS), BF16),
                   jax.ShapeDtypeStruct((depth, d, LANES), BF16)),
        grid=(depth, d // rows),
        in_specs=[pl.BlockSpec((None, rows, head_cols), lambda l, r: (l, r, 0)),
                  pl.BlockSpec((None, rows, cols - head_cols), lambda l, r: (l, r, 0))],
        out_specs=(pl.BlockSpec((None, rows, PROJ_W_COLS), lambda l, r: (l, r, 0)),
                   pl.BlockSpec((None, rows, LANES), lambda l, r: (l, r, 0))),
        compiler_params=_params(("parallel", "parallel"), 40),
        name="pack_w_in",
    )(head, tail)


def _stacked_weights(w):
    stacks = {k: w[k].astype(BF16) for k in (
        "ffn1_w_gate", "ffn1_w_up", "ffn1_w_down", "ffn2_w_gate", "ffn2_w_up", "ffn2_w_down",
        "w_branch_ssd", "w_branch_gmlp", "w_out", "ple_w_gate", "ple_w_proj")}
    stacks["w_in"], stacks["w_dt"] = _pack_w_in(w["w_in"])
    return stacks


def _layer_weights(i, w):
    pad_heads = lambda v: jnp.pad(v[i].reshape(1, SSD_HEADS), ((0, 0), (0, LANES - SSD_HEADS)))
    rowvec = lambda v: v[i].reshape(1, -1)
    head_of_channel = jnp.arange(D_INNER, dtype=jnp.int32) // SSD_HEAD_DIM
    head_rep = (jnp.arange(LANES, dtype=jnp.int32)[:, None] == head_of_channel[None, :]).astype(BF16)
    return {
        "ffn1_norm": rowvec(w["ffn1_norm"]),
        "mix_norm": rowvec(w["mix_norm"]),
        "conv_w": w["conv_w"][i],
        "conv_b": rowvec(w["conv_b"]),
        "dt_bias": pad_heads(w["dt_bias"]),
        "a_log": pad_heads(w["a_log"]),
        "d_rep": jnp.repeat(w["d_skip"][i], SSD_HEAD_DIM).reshape(1, D_INNER),
        "ssd_norm": rowvec(w["ssd_norm"]),
        "head_rep": head_rep,
        "gm_ln_g": rowvec(w["gm_ln_g"]),
        "gm_ln_b": rowvec(w["gm_ln_b"]),
        "gm_ws": w["gm_ws"][i],
        "gm_bs_t": w["gm_bs"][i].T,
        "ffn2_norm": rowvec(w["ffn2_norm"]),
        "ple_norm": rowvec(w["ple_norm"]),
    }


def _trunk(x, pe, conv_state, ssm_state, sw, layers, final_norm, keep_v):
    nseq, seqlen, d = x.shape
    depth = len(layers)
    t = nseq * seqlen
    x = x.reshape(t, d)
    new_conv, new_ssm, new_v = [], [], []
    for i, lw in enumerate(layers):
        x = _ffn(x, lw["ffn1_norm"], sw["ffn1_w_gate"], sw["ffn1_w_up"], sw["ffn1_w_down"], i)
        proj, dt_raw = _inproj(x, lw["mix_norm"], sw["w_in"], sw["w_dt"], i)
        ya, conv_out, ssm_out = _ssd(proj, dt_raw, conv_state[i], ssm_state[i], lw, nseq, seqlen)
        x, vn = _merge(ya, proj, x, lw, sw["w_branch_ssd"], sw["w_branch_gmlp"], sw["w_out"], i, seqlen, keep_v)
        x = _ffn(x, lw["ffn2_norm"], sw["ffn2_w_gate"], sw["ffn2_w_up"], sw["ffn2_w_down"], i)
        x = _ple(x, pe[i].reshape(t, -1), lw["ple_norm"], sw["ple_w_gate"], sw["ple_w_proj"],
                 final_norm if i == depth - 1 else None, i)
        new_conv.append(conv_out)
        new_ssm.append(ssm_out)
        if keep_v:
            new_v.append(vn.reshape(nseq, seqlen, GM_WIDTH))
    y = x.reshape(nseq, seqlen, d)
    return y, jnp.stack(new_conv), jnp.stack(new_ssm), (jnp.stack(new_v) if keep_v else None)


def kernel(x_prompt, x_sample, p_prompt, p_sample, state_conv, state_ssm, ffn1_norm, ffn1_w_gate, ffn1_w_up, ffn1_w_down, mix_norm, w_in, conv_w, conv_b, dt_bias, a_log, d_skip, ssd_norm, gm_ln_g, gm_ln_b, gm_ws, gm_bs, w_branch_ssd, w_branch_gmlp, w_out, ffn2_norm, ffn2_w_gate, ffn2_w_up, ffn2_w_down, ple_norm, ple_w_gate, ple_w_proj, final_norm):
    w = {
        "ffn1_norm": ffn1_norm, "ffn1_w_gate": ffn1_w_gate, "ffn1_w_up": ffn1_w_up, "ffn1_w_down": ffn1_w_down,
        "mix_norm": mix_norm, "w_in": w_in, "conv_w": conv_w, "conv_b": conv_b,
        "dt_bias": dt_bias, "a_log": a_log, "d_skip": d_skip, "ssd_norm": ssd_norm,
        "gm_ln_g": gm_ln_g, "gm_ln_b": gm_ln_b, "gm_ws": gm_ws, "gm_bs": gm_bs,
        "w_branch_ssd": w_branch_ssd, "w_branch_gmlp": w_branch_gmlp, "w_out": w_out,
        "ffn2_norm": ffn2_norm, "ffn2_w_gate": ffn2_w_gate, "ffn2_w_up": ffn2_w_up, "ffn2_w_down": ffn2_w_down,
        "ple_norm": ple_norm, "ple_w_gate": ple_w_gate, "ple_w_proj": ple_w_proj,
    }
    depth = w_in.shape[0]
    layers = [_layer_weights(i, w) for i in range(depth)]
    sw = _stacked_weights(w)
    fnorm = final_norm.reshape(1, -1)
    bp = x_prompt.shape[0]
    conv0 = jnp.zeros((depth, bp, CONV_W - 1, XBC_DIM), F32)
    ssm0 = jnp.zeros((depth, bp, SSD_HEADS, SSD_HEAD_DIM, D_STATE), F32)
    y_prompt, conv_prompt, ssm_prompt, _ = _trunk(x_prompt, p_prompt, conv0, ssm0, sw, layers, fnorm, False)
    y_sample, conv_sample, ssm_sample, v_sample = _trunk(x_sample, p_sample, state_conv, state_ssm, sw, layers, fnorm, True)
    return (y_prompt, y_sample, ssm_prompt, conv_prompt, ssm_sample, conv_sample, v_sample)
```

```python
import functools

import jax
import jax.numpy as jnp
from jax import lax
from jax.experimental import pallas as pl
from jax.experimental.pallas import tpu as pltpu

F32 = jnp.float32
BF16 = jnp.bfloat16

D_MODEL = 2048
SSD_HEADS = 32
SSD_HEAD_DIM = 64
D_INNER = SSD_HEADS * SSD_HEAD_DIM
SSD_GROUPS = 4
HEADS_PER_GROUP = SSD_HEADS // SSD_GROUPS
D_STATE = 128
CONV_W = 4
BC_DIM = SSD_GROUPS * D_STATE
XBC_DIM = D_INNER + 2 * BC_DIM
GM_CHUNK = 128
GM_GROUPS = 8
GM_GROUP_DIM = 128
GM_WIDTH = GM_GROUPS * GM_GROUP_DIM
EPS = 1e-6

LANES = 128
SUBLANES = 8
MIB = 1024 * 1024

PROJ_TN = 1024
XBC_OFF = 0
Z_OFF = 4096
UV_OFF = 6144
GATE_OFF = 8192
PROJ_COLS = 12288
PROJ_PAD_TILE = XBC_DIM // PROJ_TN
PROJ_W_COLS = XBC_DIM + D_INNER + 2 * GM_WIDTH + 2 * D_MODEL

SSD_CHUNK = 128
BF16_ROWS = 2 * SUBLANES
CONV_PAD_ROWS = BF16_ROWS


def _params(semantics, vmem_mib):
    return pltpu.CompilerParams(dimension_semantics=semantics, vmem_limit_bytes=vmem_mib * MIB)


def _rmsnorm(x, g):
    return x * lax.rsqrt(jnp.mean(x * x, axis=-1, keepdims=True) + EPS) * g


def _dot(a, b):
    return jnp.dot(a, b, preferred_element_type=F32)


def _row_tile(t, want):
    return want if t % want == 0 else t


def _ffn_body(x_ref, g_ref, wg_ref, wu_ref, wd_ref, o_ref, h_ref):
    j = pl.program_id(1)

    @pl.when(j == 0)
    def _():
        x = x_ref[...]
        h_ref[...] = _rmsnorm(x, g_ref[...]).astype(BF16)
        o_ref[...] = x

    h = h_ref[...]
    act = (0.5 * jax.nn.silu(_dot(h, wg_ref[...])) * _dot(h, wu_ref[...])).astype(BF16)
    o_ref[...] += _dot(act, wd_ref[...])


def _ffn(x, g, wg, wu, wd, layer):
    t, d = x.shape
    dff = wg.shape[-1]
    tm = _row_tile(t, 1024)
    tf = 512
    return pl.pallas_call(
        _ffn_body,
        out_shape=jax.ShapeDtypeStruct((t, d), F32),
        grid=(t // tm, dff // tf),
        in_specs=[
            pl.BlockSpec((tm, d), lambda i, j: (i, 0)),
            pl.BlockSpec((1, d), lambda i, j: (0, 0)),
            pl.BlockSpec((None, d, tf), lambda i, j: (layer, 0, j)),
            pl.BlockSpec((None, d, tf), lambda i, j: (layer, 0, j)),
            pl.BlockSpec((None, tf, d), lambda i, j: (layer, j, 0)),
        ],
        out_specs=pl.BlockSpec((tm, d), lambda i, j: (i, 0)),
        scratch_shapes=[pltpu.VMEM((tm, d), BF16)],
        compiler_params=_params(("parallel", "arbitrary"), 58),
        name="ffn",
    )(x, g, wg, wu, wd)


def _dot_nt(a, b_t):
    return lax.dot_general(a, b_t, (((1,), (1,)), ((), ())), preferred_element_type=F32)


def _inproj_body(x_ref, g_ref, wt_ref, wdt_t_ref, o_ref, dt_ref, h_ref):
    @pl.when(pl.program_id(1) == 0)
    def _():
        h = _rmsnorm(x_ref[...], g_ref[...]).astype(BF16)
        h_ref[...] = h
        dt_ref[...] = _dot_nt(h, wdt_t_ref[...])

    o_ref[...] = _dot_nt(h_ref[...], wt_ref[0]).astype(o_ref.dtype)


def _w_in_row_of_tile(j):
    xbc_tiles = XBC_DIM // PROJ_TN
    z_tiles = D_INNER // PROJ_TN
    after_dt = D_INNER + XBC_DIM + SSD_HEADS
    return jnp.where(j < xbc_tiles, D_INNER + j * PROJ_TN,
                     jnp.where(j < xbc_tiles + z_tiles, (j - xbc_tiles) * PROJ_TN,
                               after_dt + (j - xbc_tiles - z_tiles) * PROJ_TN))


def _inproj(x, g, w_t, wdt_t, layer):
    t, d = x.shape
    tm = _row_tile(t, 1024)
    n_tiles = PROJ_W_COLS // PROJ_TN
    return pl.pallas_call(
        _inproj_body,
        out_shape=(jax.ShapeDtypeStruct((t, PROJ_COLS), BF16), jax.ShapeDtypeStruct((t, LANES), F32)),
        grid=(t // tm, n_tiles),
        in_specs=[
            pl.BlockSpec((tm, d), lambda i, j: (i, 0)),
            pl.BlockSpec((1, d), lambda i, j: (0, 0)),
            pl.BlockSpec((pl.Element(1), pl.Element(PROJ_TN), pl.Element(d)),
                         lambda i, j: (layer, pl.multiple_of(_w_in_row_of_tile(j), BF16_ROWS), 0)),
            pl.BlockSpec((None, LANES, d), lambda i, j: (layer, 0, 0)),
        ],
        out_specs=(
            pl.BlockSpec((tm, PROJ_TN), lambda i, j: (i, jnp.where(j >= PROJ_PAD_TILE, j + 1, j))),
            pl.BlockSpec((tm, LANES), lambda i, j: (i, 0)),
        ),
        scratch_shapes=[pltpu.VMEM((tm, d), BF16)],
        compiler_params=_params(("parallel", "arbitrary"), 52),
        name="inproj",
    )(x, g, w_t, wdt_t)


def _split_bf16(v, n):
    parts = []
    rest = v
    for _ in range(n):
        p = rest.astype(BF16)
        parts.append(p)
        rest = rest - p.astype(F32)
    return parts


def _dot_split(a_f32, b_bf16, n):
    parts = _split_bf16(a_f32, n)
    out = _dot(parts[0], b_bf16)
    for p in parts[1:]:
        out = out + _dot(p, b_bf16)
    return out


def _dot_split_lhs01(lhs01_bf16, rhs_f32, n):
    parts = _split_bf16(rhs_f32, n)
    out = _dot(lhs01_bf16, parts[0])
    for p in parts[1:]:
        out = out + _dot(lhs01_bf16, p)
    return out


def _transpose_rows_to_lanes(x):
    q = x.shape[0]
    if q == LANES:
        return x.T
    pad = jnp.zeros((LANES - q, x.shape[1]), x.dtype)
    return jnp.concatenate([x, pad], axis=0).T[:, :q]


def _shift_matrices(q):
    t = jnp.arange(q, dtype=jnp.int32)[:, None]
    j = jnp.arange(2 * q, dtype=jnp.int32)[None, :]
    return jnp.concatenate([(j == t + (q - (CONV_W - 1) + k)) for k in range(CONV_W - 1)], axis=0).astype(BF16)


def _conv_silu_chunk(cur, prev_ref, shift_ref, cw_ref, cb_ref, q):
    both = jnp.concatenate([prev_ref[...], cur], axis=0)
    shifted = _dot(shift_ref[...], both)
    acc = cb_ref[...]
    for k in range(CONV_W - 1):
        acc = acc + shifted[k * q:(k + 1) * q] * cw_ref[k:k + 1, :]
    acc = acc + cur.astype(F32) * cw_ref[CONV_W - 1:CONV_W, :]
    prev_ref[...] = cur
    return jax.nn.silu(acc)


def _ssd_body(xbc_ref, z_ref, dt_ref, conv0_ref, ssm0_ref, cw_ref, cb_ref, dtb_ref, alog_ref, drep_ref,
              ng_ref, rep_ref, shift_ref, tri_ref, ya_ref, convo_ref, ssmo_ref, prev_ref, tail_ref, st_ref,
              *, q, nc):
    c = pl.program_id(1)
    pairs_per_group = HEADS_PER_GROUP // 2
    gw = HEADS_PER_GROUP * SSD_HEAD_DIM

    @pl.when(c == 0)
    def _():
        prev_ref[...] = jnp.zeros_like(prev_ref)
        prev_ref[q - CONV_PAD_ROWS:q, :] = conv0_ref[0].astype(BF16)
        for k in range(SSD_HEADS // 2):
            g, kk = divmod(k, pairs_per_group)
            st_ref[g, :, kk * LANES:(kk + 1) * LANES] = ssm0_ref[0, k].T

    xc = _conv_silu_chunk(xbc_ref[...], prev_ref, shift_ref, cw_ref, cb_ref, q)
    xs = xc[:, :D_INNER]
    xs_b = xs.astype(BF16)

    dt = jax.nn.softplus(dt_ref[...] + dtb_ref[...])
    d_a = dt * (-jnp.exp(alog_ref[...]))
    row = lax.broadcasted_iota(jnp.int32, (q, q), 0)
    col = lax.broadcasted_iota(jnp.int32, (q, q), 1)
    tril = row >= col
    cum = _dot_split_lhs01(tri_ref[...], d_a, 3)
    cum_last = cum[q - 1:q, :]
    ecum = jnp.exp(cum)
    dec_end = jnp.exp(cum_last - cum)
    cdec = jnp.broadcast_to(jnp.exp(cum_last), (BF16_ROWS, LANES))

    rep = rep_ref[...]
    expanded = _dot_split(jnp.concatenate([ecum, dt * dec_end], axis=0), rep, 2)
    ecum_e = expanded[:q]
    dtdec_e = expanded[q:2 * q]
    cdec_e = _dot_split(cdec, rep, 2)[:1]

    xdtd_b = (xs * dtdec_e).astype(BF16)
    src_t = _transpose_rows_to_lanes(cum - jnp.log(dt))
    left = lax.broadcasted_iota(jnp.int32, (q, LANES), 1) < SSD_HEAD_DIM

    def masked_decay(cb, h):
        seg = cum[:, h:h + 1] - src_t[h:h + 1, :]
        return (cb * jnp.exp(jnp.where(tril, seg, -jnp.inf))).astype(BF16)

    ys = []
    for g in range(SSD_GROUPS):
        b_f32 = xc[:, D_INNER + g * D_STATE:D_INNER + (g + 1) * D_STATE]
        b_g = b_f32.astype(BF16)
        c_g = xc[:, D_INNER + BC_DIM + g * D_STATE:D_INNER + BC_DIM + (g + 1) * D_STATE].astype(BF16)
        cb = lax.dot_general(c_g, b_g, (((1,), (1,)), ((), ())), preferred_element_type=F32)
        st_g = st_ref[g]
        y_off = _dot(c_g, st_g.astype(BF16)) * ecum_e[:, g * gw:(g + 1) * gw]
        for kk in range(pairs_per_group):
            pair = g * pairs_per_group + kk
            m0 = masked_decay(cb, 2 * pair)
            m1 = masked_decay(cb, 2 * pair + 1)
            xpair = xs_b[:, pair * LANES:(pair + 1) * LANES]
            if q % LANES == 0:
                x_l = jnp.where(left, xpair, jnp.zeros_like(xpair))
                x_r = jnp.where(left, jnp.zeros_like(xpair), xpair)
                y_diag = _dot(jnp.concatenate([m0, m1], axis=1), jnp.concatenate([x_l, x_r], axis=0))
            else:
                y_diag = jnp.where(left, _dot(m0, xpair), _dot(m1, xpair))
            ys.append(y_diag + y_off[:, kk * LANES:(kk + 1) * LANES])
        b_t = _transpose_rows_to_lanes(b_f32).astype(BF16)
        st_ref[g] = st_g * cdec_e[:, g * gw:(g + 1) * gw] + _dot(b_t, xdtd_b[:, g * gw:(g + 1) * gw])

    y = jnp.concatenate(ys, axis=1) + xs * drep_ref[...]

    yg = y * jax.nn.silu(z_ref[...].astype(F32))
    normed = []
    for g in range(SSD_GROUPS):
        blk = yg[:, g * gw:(g + 1) * gw]
        normed.append(blk * lax.rsqrt(jnp.mean(blk * blk, axis=-1, keepdims=True) + EPS))
    ya_ref[...] = (jnp.concatenate(normed, axis=1) * ng_ref[...]).astype(BF16)

    @pl.when(c == nc - 1)
    def _():
        tail_ref[...] = xbc_ref[q - CONV_PAD_ROWS:q, :].astype(F32)
        convo_ref[0] = tail_ref[CONV_PAD_ROWS - (CONV_W - 1):CONV_PAD_ROWS, :]
        for k in range(SSD_HEADS // 2):
            g, kk = divmod(k, pairs_per_group)
            ssmo_ref[0, k] = st_ref[g, :, kk * LANES:(kk + 1) * LANES].T


def _ssd(proj, dt_raw, conv0, ssm0, lw, nseq, seqlen):
    t = proj.shape[0]
    q = min(SSD_CHUNK, seqlen)
    nc = seqlen // q
    ssm0_pairs = ssm0.reshape(nseq, SSD_HEADS // 2, 2 * SSD_HEAD_DIM, D_STATE)
    conv0 = jnp.pad(conv0, ((0, 0), (CONV_PAD_ROWS - (CONV_W - 1), 0), (0, 0)))
    const = lambda b, c: (0, 0)
    rowblk = lambda b, c: (b * nc + c, 0)
    ya, conv_out, ssm_out = pl.pallas_call(
        functools.partial(_ssd_body, q=q, nc=nc),
        out_shape=(
            jax.ShapeDtypeStruct((t, D_INNER), BF16),
            jax.ShapeDtypeStruct((nseq, CONV_W - 1, XBC_DIM), F32),
            jax.ShapeDtypeStruct(ssm0_pairs.shape, F32),
        ),
        grid=(nseq, nc),
        in_specs=[
            pl.BlockSpec((q, XBC_DIM), rowblk),
            pl.BlockSpec((q, D_INNER), lambda b, c: (b * nc + c, Z_OFF // D_INNER)),
            pl.BlockSpec((q, LANES), rowblk),
            pl.BlockSpec((1, CONV_PAD_ROWS, XBC_DIM), lambda b, c: (b, 0, 0)),
            pl.BlockSpec((1,) + ssm0_pairs.shape[1:], lambda b, c: (b, 0, 0, 0)),
            pl.BlockSpec((CONV_W, XBC_DIM), const),
            pl.BlockSpec((1, XBC_DIM), const),
            pl.BlockSpec((1, LANES), const),
            pl.BlockSpec((1, LANES), const),
            pl.BlockSpec((1, D_INNER), const),
            pl.BlockSpec((1, D_INNER), const),
            pl.BlockSpec((LANES, D_INNER), const),
            pl.BlockSpec(((CONV_W - 1) * q, 2 * q), const),
            pl.BlockSpec((q, q), const),
        ],
        out_specs=(
            pl.BlockSpec((q, D_INNER), rowblk),
            pl.BlockSpec((1, CONV_W - 1, XBC_DIM), lambda b, c: (b, 0, 0)),
            pl.BlockSpec((1,) + ssm0_pairs.shape[1:], lambda b, c: (b, 0, 0, 0)),
        ),
        scratch_shapes=[
            pltpu.VMEM((q, XBC_DIM), BF16),
            pltpu.VMEM((CONV_PAD_ROWS, XBC_DIM), F32),
            pltpu.VMEM((SSD_GROUPS, D_STATE, HEADS_PER_GROUP * SSD_HEAD_DIM), F32),
        ],
        compiler_params=_params(("parallel", "arbitrary"), 48),
        name="ssd",
    )(proj, proj, dt_raw, conv0, ssm0_pairs, lw["conv_w"], lw["conv_b"], lw["dt_bias"], lw["a_log"],
      lw["d_rep"], lw["ssd_norm"], lw["head_rep"], _shift_matrices(q), jnp.tri(q, dtype=BF16))
    return ya, conv_out, ssm_out.reshape(ssm0.shape)


def _gmlp_rows(uv_ref, lg_ref, lb_ref, ws_ref, bst_ref, vn_ref, tc, nchunks):
    uv = jax.nn.gelu(uv_ref[...].astype(F32))
    u = uv[:, :GM_WIDTH]
    v = uv[:, GM_WIDTH:]
    mu = jnp.mean(v, axis=-1, keepdims=True)
    dev = v - mu
    var = jnp.mean(dev * dev, axis=-1, keepdims=True)
    vn = dev * lax.rsqrt(var + EPS) * lg_ref[...] + lb_ref[...]
    if vn_ref is not None:
        vn_ref[...] = vn
    vb = vn.astype(BF16)

    row = lax.broadcasted_iota(jnp.int32, (tc, tc), 0)
    col = lax.broadcasted_iota(jnp.int32, (tc, tc), 1)
    tril = row >= col
    mixed = [[None] * GM_GROUPS for _ in range(nchunks)]
    for g in range(GM_GROUPS):
        w = jnp.where(tril, ws_ref[g, :tc, :tc], 0.0).astype(BF16)
        bias = bst_ref[:tc, g:g + 1]
        cols = slice(g * GM_GROUP_DIM, (g + 1) * GM_GROUP_DIM)
        if tc == LANES:
            rhs = jnp.concatenate([vb[r * tc:(r + 1) * tc, cols] for r in range(nchunks)], axis=1)
            out = _dot(w, rhs) + bias
            for r in range(nchunks):
                mixed[r][g] = out[:, r * GM_GROUP_DIM:(r + 1) * GM_GROUP_DIM]
        else:
            for r in range(nchunks):
                mixed[r][g] = _dot(w, vb[r * tc:(r + 1) * tc, cols]) + bias
    full = jnp.concatenate([jnp.concatenate(m, axis=1) for m in mixed], axis=0)
    return (u * full).astype(BF16)


def _merge_body(ya_ref, uv_ref, ga_ref, gb_ref, x_ref, lg_ref, lb_ref, ws_ref, bst_ref, wa_ref, wb_ref, wo_ref,
                o_ref, *maybe_vn_ref, tc, nchunks):
    pa = _dot(ya_ref[...], wa_ref[...])
    yb = _gmlp_rows(uv_ref, lg_ref, lb_ref, ws_ref, bst_ref, maybe_vn_ref[0] if maybe_vn_ref else None, tc, nchunks)
    merged = (jax.nn.sigmoid(ga_ref[...].astype(F32)) * pa
              + jax.nn.sigmoid(gb_ref[...].astype(F32)) * _dot(yb, wb_ref[...]))
    o_ref[...] = x_ref[...] + _dot(merged.astype(BF16), wo_ref[...])


def _resident(block_shape, index_map):
    return pl.BlockSpec(block_shape, index_map, pipeline_mode=pl.Buffered(1))


def _merge(ya, proj, x, lw, wa, wb, wo, layer, seqlen, keep_v):
    t, d = x.shape
    tm = _row_tile(t, 256)
    tc = min(GM_CHUNK, seqlen)
    ga = GATE_OFF // d
    whole = lambda i: (layer, 0, 0)
    const = lambda i: (0, 0)
    out_shape = [jax.ShapeDtypeStruct((t, d), F32)]
    out_specs = [pl.BlockSpec((tm, d), lambda i: (i, 0))]
    if keep_v:
        out_shape.append(jax.ShapeDtypeStruct((t, GM_WIDTH), F32))
        out_specs.append(pl.BlockSpec((tm, GM_WIDTH), lambda i: (i, 0)))
    outs = pl.pallas_call(
        functools.partial(_merge_body, tc=tc, nchunks=tm // tc),
        out_shape=tuple(out_shape),
        grid=(t // tm,),
        in_specs=[
            pl.BlockSpec((tm, D_INNER), lambda i: (i, 0)),
            pl.BlockSpec((tm, 2 * GM_WIDTH), lambda i: (i, UV_OFF // (2 * GM_WIDTH))),
            pl.BlockSpec((tm, d), lambda i: (i, ga)),
            pl.BlockSpec((tm, d), lambda i: (i, ga + 1)),
            pl.BlockSpec((tm, d), lambda i: (i, 0)),
            pl.BlockSpec((1, GM_WIDTH), const),
            pl.BlockSpec((1, GM_WIDTH), const),
            pl.BlockSpec((GM_GROUPS, GM_CHUNK, GM_CHUNK), lambda i: (0, 0, 0)),
            pl.BlockSpec((GM_CHUNK, GM_GROUPS), const),
            _resident((None, D_INNER, d), whole),
            _resident((None, GM_WIDTH, d), whole),
            _resident((None, d, d), whole),
        ],
        out_specs=tuple(out_specs),
        compiler_params=_params(("parallel",), 56),
        name="merge",
    )(ya, proj, proj, proj, x, lw["gm_ln_g"], lw["gm_ln_b"], lw["gm_ws"], lw["gm_bs_t"], wa, wb, wo)
    return (outs[0], outs[1]) if keep_v else (outs[0], None)


def _ple_body(x_ref, pe_ref, g_ref, wg_ref, wp_ref, *rest, final):
    o_ref = rest[-1]
    x = x_ref[...]
    h = _rmsnorm(x, g_ref[...]).astype(BF16)
    out = x + _dot(pe_ref[...].astype(BF16), wp_ref[...]) * jax.nn.sigmoid(_dot(h, wg_ref[...]))
    if final:
        out = _rmsnorm(out, rest[0][...])
    o_ref[...] = out


def _ple(x, pe, g, wg, wp, final_g, layer):
    t, d = x.shape
    tm = _row_tile(t, 512)
    final = final_g is not None
    const = lambda i: (0, 0)
    whole = lambda i: (layer, 0, 0)
    in_specs = [
        pl.BlockSpec((tm, d), lambda i: (i, 0)),
        pl.BlockSpec((tm, pe.shape[1]), lambda i: (i, 0)),
        pl.BlockSpec((1, d), const),
        _resident((None, d, d), whole),
        _resident((None, pe.shape[1], d), whole),
    ]
    args = [x, pe, g, wg, wp]
    if final:
        in_specs.append(pl.BlockSpec((1, d), const))
        args.append(final_g)
    return pl.pallas_call(
        functools.partial(_ple_body, final=final),
        out_shape=jax.ShapeDtypeStruct((t, d), F32),
        grid=(t // tm,),
        in_specs=in_specs,
        out_specs=pl.BlockSpec((tm, d), lambda i: (i, 0)),
        compiler_params=_params(("parallel",), 48),
        name="ple_final" if final else "ple",
    )(*args)


def _stacked_weights(w):
    stacks = {k: w[k].astype(BF16) for k in (
        "ffn1_w_gate", "ffn1_w_up", "ffn1_w_down", "ffn2_w_gate", "ffn2_w_up", "ffn2_w_down",
        "w_branch_ssd", "w_branch_gmlp", "w_out", "ple_w_gate", "ple_w_proj")}
    w_in_t = jnp.swapaxes(w["w_in"], 1, 2).astype(BF16)
    dt_lo = D_INNER + XBC_DIM
    stacks["w_in_t"] = w_in_t
    stacks["w_dt_t"] = jnp.pad(w_in_t[:, dt_lo:dt_lo + SSD_HEADS, :], ((0, 0), (0, LANES - SSD_HEADS), (0, 0)))
    return stacks


def _layer_weights(i, w):
    pad_heads = lambda v: jnp.pad(v[i].reshape(1, SSD_HEADS), ((0, 0), (0, LANES - SSD_HEADS)))
    rowvec = lambda v: v[i].reshape(1, -1)
    head_of_channel = jnp.arange(D_INNER, dtype=jnp.int32) // SSD_HEAD_DIM
    head_rep = (jnp.arange(LANES, dtype=jnp.int32)[:, None] == head_of_channel[None, :]).astype(BF16)
    return {
        "ffn1_norm": rowvec(w["ffn1_norm"]),
        "mix_norm": rowvec(w["mix_norm"]),
        "conv_w": w["conv_w"][i],
        "conv_b": rowvec(w["conv_b"]),
        "dt_bias": pad_heads(w["dt_bias"]),
        "a_log": pad_heads(w["a_log"]),
        "d_rep": jnp.repeat(w["d_skip"][i], SSD_HEAD_DIM).reshape(1, D_INNER),
        "ssd_norm": rowvec(w["ssd_norm"]),
        "head_rep": head_rep,
        "gm_ln_g": rowvec(w["gm_ln_g"]),
        "gm_ln_b": rowvec(w["gm_ln_b"]),
        "gm_ws": w["gm_ws"][i],
        "gm_bs_t": w["gm_bs"][i].T,
        "ffn2_norm": rowvec(w["ffn2_norm"]),
        "ple_norm": rowvec(w["ple_norm"]),
    }


def _trunk(x, pe, conv_state, ssm_state, sw, layers, final_norm, keep_v):
    nseq, seqlen, d = x.shape
    depth = len(layers)
    t = nseq * seqlen
    x = x.reshape(t, d)
    new_conv, new_ssm, new_v = [], [], []
    for i, lw in enumerate(layers):
        x = _ffn(x, lw["ffn1_norm"], sw["ffn1_w_gate"], sw["ffn1_w_up"], sw["ffn1_w_down"], i)
        proj, dt_raw = _inproj(x, lw["mix_norm"], sw["w_in_t"], sw["w_dt_t"], i)
        ya, conv_out, ssm_out = _ssd(proj, dt_raw, conv_state[i], ssm_state[i], lw, nseq, seqlen)
        x, vn = _merge(ya, proj, x, lw, sw["w_branch_ssd"], sw["w_branch_gmlp"], sw["w_out"], i, seqlen, keep_v)
        x = _ffn(x, lw["ffn2_norm"], sw["ffn2_w_gate"], sw["ffn2_w_up"], sw["ffn2_w_down"], i)
        x = _ple(x, pe[i].reshape(t, -1), lw["ple_norm"], sw["ple_w_gate"], sw["ple_w_proj"],
                 final_norm if i == depth - 1 else None, i)
        new_conv.append(conv_out)
        new_ssm.append(ssm_out)
        if keep_v:
            new_v.append(vn.reshape(nseq, seqlen, GM_WIDTH))
    y = x.reshape(nseq, seqlen, d)
    return y, jnp.stack(new_conv), jnp.stack(new_ssm), (jnp.stack(new_v) if keep_v else None)


def kernel(x_prompt, x_sample, p_prompt, p_sample, state_conv, state_ssm, ffn1_norm, ffn1_w_gate, ffn1_w_up, ffn1_w_down, mix_norm, w_in, conv_w, conv_b, dt_bias, a_log, d_skip, ssd_norm, gm_ln_g, gm_ln_b, gm_ws, gm_bs, w_branch_ssd, w_branch_gmlp, w_out, ffn2_norm, ffn2_w_gate, ffn2_w_up, ffn2_w_down, ple_norm, ple_w_gate, ple_w_proj, final_norm):
    w = {
        "ffn1_norm": ffn1_norm, "ffn1_w_gate": ffn1_w_gate, "ffn1_w_up": ffn1_w_up, "ffn1_w_down": ffn1_w_down,
        "mix_norm": mix_norm, "w_in": w_in, "conv_w": conv_w, "conv_b": conv_b,
        "dt_bias": dt_bias, "a_log": a_log, "d_skip": d_skip, "ssd_norm": ssd_norm,
        "gm_ln_g": gm_ln_g, "gm_ln_b": gm_ln_b, "gm_ws": gm_ws, "gm_bs": gm_bs,
        "w_branch_ssd": w_branch_ssd, "w_branch_gmlp": w_branch_gmlp, "w_out": w_out,
        "ffn2_norm": ffn2_norm, "ffn2_w_gate": ffn2_w_gate, "ffn2_w_up": ffn2_w_up, "ffn2_w_down": ffn2_w_down,
        "ple_norm": ple_norm, "ple_w_gate": ple_w_gate, "ple_w_proj": ple_w_proj,
    }
    depth = w_in.shape[0]
    layers = [_layer_weights(i, w) for i in range(depth)]
    sw = _stacked_weights(w)
    fnorm = final_norm.reshape(1, -1)
    bp = x_prompt.shape[0]
    conv0 = jnp.zeros((depth, bp, CONV_W - 1, XBC_DIM), F32)
    ssm0 = jnp.zeros((depth, bp, SSD_HEADS, SSD_HEAD_DIM, D_STATE), F32)
    y_prompt, conv_prompt, ssm_prompt, _ = _trunk(x_prompt, p_prompt, conv0, ssm0, sw, layers, fnorm, False)
    y_sample, conv_sample, ssm_sample, v_sample = _trunk(x_sample, p_sample, state_conv, state_ssm, sw, layers, fnorm, True)
    return (y_prompt, y_sample, ssm_prompt, conv_prompt, ssm_sample, conv_sample, v_sample)
```

```python
import functools

import jax
import jax.numpy as jnp
from jax import lax
from jax.experimental import pallas as pl
from jax.experimental.pallas import tpu as pltpu

F32 = jnp.float32
BF16 = jnp.bfloat16

D_MODEL = 2048
SSD_HEADS = 32
SSD_HEAD_DIM = 64
D_INNER = SSD_HEADS * SSD_HEAD_DIM
SSD_GROUPS = 4
HEADS_PER_GROUP = SSD_HEADS // SSD_GROUPS
D_STATE = 128
CONV_W = 4
BC_DIM = SSD_GROUPS * D_STATE
XBC_DIM = D_INNER + 2 * BC_DIM
GM_CHUNK = 128
GM_GROUPS = 8
GM_GROUP_DIM = 128
GM_WIDTH = GM_GROUPS * GM_GROUP_DIM
EPS = 1e-6

LANES = 128
SUBLANES = 8
MIB = 1024 * 1024

PROJ_TN = 1024
XBC_OFF = 0
Z_OFF = 4096
UV_OFF = 6144
GATE_OFF = 8192
PROJ_COLS = 12288
PROJ_PAD_TILE = XBC_DIM // PROJ_TN
PROJ_W_COLS = XBC_DIM + D_INNER + 2 * GM_WIDTH + 2 * D_MODEL

SSD_CHUNK = 128
BF16_ROWS = 2 * SUBLANES
CONV_PAD_ROWS = BF16_ROWS


def _params(semantics, vmem_mib):
    return pltpu.CompilerParams(dimension_semantics=semantics, vmem_limit_bytes=vmem_mib * MIB)


def _rmsnorm(x, g):
    return x * lax.rsqrt(jnp.mean(x * x, axis=-1, keepdims=True) + EPS) * g


def _dot(a, b):
    return jnp.dot(a, b, preferred_element_type=F32)


def _row_tile(t, want):
    return want if t % want == 0 else t


def _ffn_body(x_ref, g_ref, wg_ref, wu_ref, wd_ref, o_ref, h_ref):
    j = pl.program_id(1)

    @pl.when(j == 0)
    def _():
        x = x_ref[...]
        h_ref[...] = _rmsnorm(x, g_ref[...]).astype(BF16)
        o_ref[...] = x

    h = h_ref[...]
    act = (0.5 * jax.nn.silu(_dot(h, wg_ref[...])) * _dot(h, wu_ref[...])).astype(BF16)
    o_ref[...] += _dot(act, wd_ref[...])


def _ffn(x, g, wg, wu, wd, layer):
    t, d = x.shape
    dff = wg.shape[-1]
    tm = _row_tile(t, 1024)
    tf = 512
    return pl.pallas_call(
        _ffn_body,
        out_shape=jax.ShapeDtypeStruct((t, d), F32),
        grid=(t // tm, dff // tf),
        in_specs=[
            pl.BlockSpec((tm, d), lambda i, j: (i, 0)),
            pl.BlockSpec((1, d), lambda i, j: (0, 0)),
            pl.BlockSpec((None, d, tf), lambda i, j: (layer, 0, j)),
            pl.BlockSpec((None, d, tf), lambda i, j: (layer, 0, j)),
            pl.BlockSpec((None, tf, d), lambda i, j: (layer, j, 0)),
        ],
        out_specs=pl.BlockSpec((tm, d), lambda i, j: (i, 0)),
        scratch_shapes=[pltpu.VMEM((tm, d), BF16)],
        compiler_params=_params(("parallel", "arbitrary"), 58),
        name="ffn",
    )(x, g, wg, wu, wd)


def _dot_nt(a, b_t):
    return lax.dot_general(a, b_t, (((1,), (1,)), ((), ())), preferred_element_type=F32)


def _inproj_body(x_ref, g_ref, wt_ref, wdt_t_ref, o_ref, dt_ref, h_ref):
    @pl.when(pl.program_id(1) == 0)
    def _():
        h = _rmsnorm(x_ref[...], g_ref[...]).astype(BF16)
        h_ref[...] = h
        dt_ref[...] = _dot_nt(h, wdt_t_ref[...])

    o_ref[...] = _dot_nt(h_ref[...], wt_ref[0]).astype(o_ref.dtype)


def _w_in_row_of_tile(j):
    xbc_tiles = XBC_DIM // PROJ_TN
    z_tiles = D_INNER // PROJ_TN
    after_dt = D_INNER + XBC_DIM + SSD_HEADS
    return jnp.where(j < xbc_tiles, D_INNER + j * PROJ_TN,
                     jnp.where(j < xbc_tiles + z_tiles, (j - xbc_tiles) * PROJ_TN,
                               after_dt + (j - xbc_tiles - z_tiles) * PROJ_TN))


def _inproj(x, g, w_t, wdt_t, layer):
    t, d = x.shape
    tm = _row_tile(t, 1024)
    n_tiles = PROJ_W_COLS // PROJ_TN
    return pl.pallas_call(
        _inproj_body,
        out_shape=(jax.ShapeDtypeStruct((t, PROJ_COLS), BF16), jax.ShapeDtypeStruct((t, LANES), F32)),
        grid=(t // tm, n_tiles),
        in_specs=[
            pl.BlockSpec((tm, d), lambda i, j: (i, 0)),
            pl.BlockSpec((1, d), lambda i, j: (0, 0)),
            pl.BlockSpec((pl.Element(1), pl.Element(PROJ_TN), pl.Element(d)),
                         lambda i, j: (layer, pl.multiple_of(_w_in_row_of_tile(j), BF16_ROWS), 0)),
            pl.BlockSpec((None, LANES, d), lambda i, j: (layer, 0, 0)),
        ],
        out_specs=(
            pl.BlockSpec((tm, PROJ_TN), lambda i, j: (i, jnp.where(j >= PROJ_PAD_TILE, j + 1, j))),
            pl.BlockSpec((tm, LANES), lambda i, j: (i, 0)),
        ),
        scratch_shapes=[pltpu.VMEM((tm, d), BF16)],
        compiler_params=_params(("parallel", "arbitrary"), 52),
        name="inproj",
    )(x, g, w_t, wdt_t)


def _split_bf16(v, n):
    parts = []
    rest = v
    for _ in range(n):
        p = rest.astype(BF16)
        parts.append(p)
        rest = rest - p.astype(F32)
    return parts


def _dot_split(a_f32, b_bf16, n):
    parts = _split_bf16(a_f32, n)
    out = _dot(parts[0], b_bf16)
    for p in parts[1:]:
        out = out + _dot(p, b_bf16)
    return out


def _dot_split_lhs01(lhs01_bf16, rhs_f32, n):
    parts = _split_bf16(rhs_f32, n)
    out = _dot(lhs01_bf16, parts[0])
    for p in parts[1:]:
        out = out + _dot(lhs01_bf16, p)
    return out


def _transpose_rows_to_lanes(x):
    q = x.shape[0]
    if q == LANES:
        return x.T
    pad = jnp.zeros((LANES - q, x.shape[1]), x.dtype)
    return jnp.concatenate([x, pad], axis=0).T[:, :q]


def _shift_matrices(q):
    t = jnp.arange(q, dtype=jnp.int32)[:, None]
    j = jnp.arange(2 * q, dtype=jnp.int32)[None, :]
    return jnp.concatenate([(j == t + (q - (CONV_W - 1) + k)) for k in range(CONV_W - 1)], axis=0).astype(BF16)


def _conv_silu_chunk(cur, prev_ref, shift_ref, cw_ref, cb_ref, q):
    both = jnp.concatenate([prev_ref[...], cur], axis=0)
    shifted = _dot(shift_ref[...], both)
    acc = cb_ref[...]
    for k in range(CONV_W - 1):
        acc = acc + shifted[k * q:(k + 1) * q] * cw_ref[k:k + 1, :]
    acc = acc + cur.astype(F32) * cw_ref[CONV_W - 1:CONV_W, :]
    prev_ref[...] = cur
    return jax.nn.silu(acc)


def _ssd_chunk(xbc_ref, z_ref, dt_ref, cw_ref, cb_ref, dtb_ref, alog_ref, drep_ref, ng_ref, rep_ref, shift_ref,
               tri_ref, ya_ref, prev_ref, st_ref, q):
    pairs_per_group = HEADS_PER_GROUP // 2
    gw = HEADS_PER_GROUP * SSD_HEAD_DIM

    xc = _conv_silu_chunk(xbc_ref[...], prev_ref, shift_ref, cw_ref, cb_ref, q)
    xs = xc[:, :D_INNER]
    xs_b = xs.astype(BF16)

    dt = jax.nn.softplus(dt_ref[...] + dtb_ref[...])
    d_a = dt * (-jnp.exp(alog_ref[...]))
    row = lax.broadcasted_iota(jnp.int32, (q, q), 0)
    col = lax.broadcasted_iota(jnp.int32, (q, q), 1)
    tril = row >= col
    cum = _dot_split_lhs01(tri_ref[...], d_a, 3)
    cum_last = cum[q - 1:q, :]
    ecum = jnp.exp(cum)
    dec_end = jnp.exp(cum_last - cum)
    cdec = jnp.broadcast_to(jnp.exp(cum_last), (BF16_ROWS, LANES))

    rep = rep_ref[...]
    expanded = _dot_split(jnp.concatenate([ecum, dt * dec_end], axis=0), rep, 2)
    ecum_e = expanded[:q]
    dtdec_e = expanded[q:2 * q]
    cdec_e = _dot_split(cdec, rep, 2)[:1]

    xdtd_b = (xs * dtdec_e).astype(BF16)
    src_t = _transpose_rows_to_lanes(cum - jnp.log(dt))
    left = lax.broadcasted_iota(jnp.int32, (q, LANES), 1) < SSD_HEAD_DIM

    def masked_decay(cb, h):
        seg = cum[:, h:h + 1] - src_t[h:h + 1, :]
        return (cb * jnp.exp(jnp.where(tril, seg, -jnp.inf))).astype(BF16)

    ys = []
    for g in range(SSD_GROUPS):
        b_f32 = xc[:, D_INNER + g * D_STATE:D_INNER + (g + 1) * D_STATE]
        b_g = b_f32.astype(BF16)
        c_g = xc[:, D_INNER + BC_DIM + g * D_STATE:D_INNER + BC_DIM + (g + 1) * D_STATE].astype(BF16)
        cb = lax.dot_general(c_g, b_g, (((1,), (1,)), ((), ())), preferred_element_type=F32)
        st_g = st_ref[g]
        y_off = _dot(c_g, st_g.astype(BF16)) * ecum_e[:, g * gw:(g + 1) * gw]
        for kk in range(pairs_per_group):
            pair = g * pairs_per_group + kk
            m0 = masked_decay(cb, 2 * pair)
            m1 = masked_decay(cb, 2 * pair + 1)
            xpair = xs_b[:, pair * LANES:(pair + 1) * LANES]
            if q % LANES == 0:
                x_l = jnp.where(left, xpair, jnp.zeros_like(xpair))
                x_r = jnp.where(left, jnp.zeros_like(xpair), xpair)
                y_diag = _dot(jnp.concatenate([m0, m1], axis=1), jnp.concatenate([x_l, x_r], axis=0))
            else:
                y_diag = jnp.where(left, _dot(m0, xpair), _dot(m1, xpair))
            ys.append(y_diag + y_off[:, kk * LANES:(kk + 1) * LANES])
        b_t = _transpose_rows_to_lanes(b_f32).astype(BF16)
        st_ref[g] = st_g * cdec_e[:, g * gw:(g + 1) * gw] + _dot(b_t, xdtd_b[:, g * gw:(g + 1) * gw])

    y = jnp.concatenate(ys, axis=1) + xs * drep_ref[...]

    yg = y * jax.nn.silu(z_ref[...].astype(F32))
    normed = []
    for g in range(SSD_GROUPS):
        blk = yg[:, g * gw:(g + 1) * gw]
        normed.append(blk * lax.rsqrt(jnp.mean(blk * blk, axis=-1, keepdims=True) + EPS))
    ya_ref[...] = (jnp.concatenate(normed, axis=1) * ng_ref[...]).astype(BF16)

def _ssd_body(xbc_ref, z_ref, dt_ref, conv0_ref, ssm0_ref, cw_ref, cb_ref, dtb_ref, alog_ref, drep_ref,
              ng_ref, rep_ref, shift_ref, tri_ref, ya_ref, convo_ref, ssmo_ref, prev_ref, tail_ref, st_ref,
              *, q, nc, spb):
    c = pl.program_id(1)
    pairs_per_group = HEADS_PER_GROUP // 2

    @pl.when(c == 0)
    def _():
        for s in range(spb):
            prev_ref[s] = jnp.zeros(prev_ref.shape[1:], BF16)
            prev_ref[s, q - CONV_PAD_ROWS:q, :] = conv0_ref[s].astype(BF16)
            for k in range(SSD_HEADS // 2):
                g, kk = divmod(k, pairs_per_group)
                st_ref[s, g, :, kk * LANES:(kk + 1) * LANES] = ssm0_ref[s, k].T

    for s in range(spb):
        _ssd_chunk(xbc_ref.at[s], z_ref.at[s], dt_ref.at[s], cw_ref, cb_ref, dtb_ref, alog_ref, drep_ref, ng_ref,
                   rep_ref, shift_ref, tri_ref, ya_ref.at[s], prev_ref.at[s], st_ref.at[s], q)

    @pl.when(c == nc - 1)
    def _():
        for s in range(spb):
            tail_ref[s] = xbc_ref[s, q - CONV_PAD_ROWS:q, :].astype(F32)
            convo_ref[s] = tail_ref[s, CONV_PAD_ROWS - (CONV_W - 1):CONV_PAD_ROWS, :]
            for k in range(SSD_HEADS // 2):
                g, kk = divmod(k, pairs_per_group)
                ssmo_ref[s, k] = st_ref[s, g, :, kk * LANES:(kk + 1) * LANES].T


def _ssd(proj, dt_raw, conv0, ssm0_all, lw, nseq, seqlen, layer):
    t = proj.shape[0]
    q = min(SSD_CHUNK, seqlen)
    nc = seqlen // q
    spb = next(n for n in ((4, 2, 1) if layer == 1 else (2, 1)) if nseq % n == 0)
    state_block = (spb, SSD_HEADS // 2, 2 * SSD_HEAD_DIM, D_STATE)
    ssm0_pairs = ssm0_all.reshape(ssm0_all.shape[0], nseq, *state_block[1:])
    conv0 = jnp.pad(conv0, ((0, 0), (CONV_PAD_ROWS - (CONV_W - 1), 0), (0, 0)))
    proj3 = proj.reshape(nseq, seqlen, PROJ_COLS)
    const = lambda b, c: (0, 0)
    rowblk = lambda b, c: (b, c, 0)
    ya, conv_out, ssm_out = pl.pallas_call(
        functools.partial(_ssd_body, q=q, nc=nc, spb=spb),
        out_shape=(
            jax.ShapeDtypeStruct((nseq, seqlen, D_INNER), BF16),
            jax.ShapeDtypeStruct((nseq, CONV_W - 1, XBC_DIM), F32),
            jax.ShapeDtypeStruct((nseq,) + state_block[1:], F32),
        ),
        grid=(nseq // spb, nc),
        in_specs=[
            pl.BlockSpec((spb, q, XBC_DIM), rowblk),
            pl.BlockSpec((spb, q, D_INNER), lambda b, c: (b, c, Z_OFF // D_INNER)),
            pl.BlockSpec((spb, q, LANES), rowblk),
            pl.BlockSpec((spb, CONV_PAD_ROWS, XBC_DIM), lambda b, c: (b, 0, 0)),
            pl.BlockSpec((None,) + state_block, lambda b, c: (layer, b, 0, 0, 0)),
            pl.BlockSpec((CONV_W, XBC_DIM), const),
            pl.BlockSpec((1, XBC_DIM), const),
            pl.BlockSpec((1, LANES), const),
            pl.BlockSpec((1, LANES), const),
            pl.BlockSpec((1, D_INNER), const),
            pl.BlockSpec((1, D_INNER), const),
            pl.BlockSpec((LANES, D_INNER), const),
            pl.BlockSpec(((CONV_W - 1) * q, 2 * q), const),
            pl.BlockSpec((q, q), const),
        ],
        out_specs=(
            pl.BlockSpec((spb, q, D_INNER), rowblk),
            pl.BlockSpec((spb, CONV_W - 1, XBC_DIM), lambda b, c: (b, 0, 0)),
            pl.BlockSpec(state_block, lambda b, c: (b, 0, 0, 0)),
        ),
        scratch_shapes=[
            pltpu.VMEM((spb, q, XBC_DIM), BF16),
            pltpu.VMEM((spb, CONV_PAD_ROWS, XBC_DIM), F32),
            pltpu.VMEM((spb, SSD_GROUPS, D_STATE, HEADS_PER_GROUP * SSD_HEAD_DIM), F32),
        ],
        compiler_params=_params(("parallel", "arbitrary"), 48),
        name="ssd",
    )(proj3, proj3, dt_raw.reshape(nseq, seqlen, LANES), conv0, ssm0_pairs, lw["conv_w"], lw["conv_b"],
      lw["dt_bias"], lw["a_log"], lw["d_rep"], lw["ssd_norm"], lw["head_rep"], _shift_matrices(q),
      jnp.tri(q, dtype=BF16))
    return ya.reshape(t, D_INNER), conv_out, ssm_out.reshape(ssm0_all.shape[1:])


def _gmlp_rows(uv_ref, lg_ref, lb_ref, ws_ref, bst_ref, vn_ref, tc, nchunks):
    uv = jax.nn.gelu(uv_ref[...].astype(F32))
    u = uv[:, :GM_WIDTH]
    v = uv[:, GM_WIDTH:]
    mu = jnp.mean(v, axis=-1, keepdims=True)
    dev = v - mu
    var = jnp.mean(dev * dev, axis=-1, keepdims=True)
    vn = dev * lax.rsqrt(var + EPS) * lg_ref[...] + lb_ref[...]
    if vn_ref is not None:
        vn_ref[...] = vn
    vb = vn.astype(BF16)

    row = lax.broadcasted_iota(jnp.int32, (tc, tc), 0)
    col = lax.broadcasted_iota(jnp.int32, (tc, tc), 1)
    tril = row >= col
    mixed = [[None] * GM_GROUPS for _ in range(nchunks)]
    for g in range(GM_GROUPS):
        w = jnp.where(tril, ws_ref[g, :tc, :tc], 0.0).astype(BF16)
        bias = bst_ref[:tc, g:g + 1]
        cols = slice(g * GM_GROUP_DIM, (g + 1) * GM_GROUP_DIM)
        if tc == LANES:
            rhs = jnp.concatenate([vb[r * tc:(r + 1) * tc, cols] for r in range(nchunks)], axis=1)
            out = _dot(w, rhs) + bias
            for r in range(nchunks):
                mixed[r][g] = out[:, r * GM_GROUP_DIM:(r + 1) * GM_GROUP_DIM]
        else:
            for r in range(nchunks):
                mixed[r][g] = _dot(w, vb[r * tc:(r + 1) * tc, cols]) + bias
    full = jnp.concatenate([jnp.concatenate(m, axis=1) for m in mixed], axis=0)
    return (u * full).astype(BF16)


def _merge_body(ya_ref, uv_ref, ga_ref, gb_ref, x_ref, lg_ref, lb_ref, ws_ref, bst_ref, wa_ref, wb_ref, wo_ref,
                o_ref, *maybe_vn_ref, tc, nchunks):
    pa = _dot(ya_ref[...], wa_ref[...])
    yb = _gmlp_rows(uv_ref, lg_ref, lb_ref, ws_ref, bst_ref, maybe_vn_ref[0] if maybe_vn_ref else None, tc, nchunks)
    merged = (jax.nn.sigmoid(ga_ref[...].astype(F32)) * pa
              + jax.nn.sigmoid(gb_ref[...].astype(F32)) * _dot(yb, wb_ref[...]))
    o_ref[...] = x_ref[...] + _dot(merged.astype(BF16), wo_ref[...])


def _resident(block_shape, index_map):
    return pl.BlockSpec(block_shape, index_map, pipeline_mode=pl.Buffered(1))


def _merge(ya, proj, x, lw, wa, wb, wo, layer, seqlen, keep_v):
    t, d = x.shape
    tm = _row_tile(t, 256)
    tc = min(GM_CHUNK, seqlen)
    ga = GATE_OFF // d
    whole = lambda i: (layer, 0, 0)
    const = lambda i: (0, 0)
    out_shape = [jax.ShapeDtypeStruct((t, d), F32)]
    out_specs = [pl.BlockSpec((tm, d), lambda i: (i, 0))]
    if keep_v:
        out_shape.append(jax.ShapeDtypeStruct((t, GM_WIDTH), F32))
        out_specs.append(pl.BlockSpec((tm, GM_WIDTH), lambda i: (i, 0)))
    outs = pl.pallas_call(
        functools.partial(_merge_body, tc=tc, nchunks=tm // tc),
        out_shape=tuple(out_shape),
        grid=(t // tm,),
        in_specs=[
            pl.BlockSpec((tm, D_INNER), lambda i: (i, 0)),
            pl.BlockSpec((tm, 2 * GM_WIDTH), lambda i: (i, UV_OFF // (2 * GM_WIDTH))),
            pl.BlockSpec((tm, d), lambda i: (i, ga)),
            pl.BlockSpec((tm, d), lambda i: (i, ga + 1)),
            pl.BlockSpec((tm, d), lambda i: (i, 0)),
            pl.BlockSpec((1, GM_WIDTH), const),
            pl.BlockSpec((1, GM_WIDTH), const),
            pl.BlockSpec((GM_GROUPS, GM_CHUNK, GM_CHUNK), lambda i: (0, 0, 0)),
            pl.BlockSpec((GM_CHUNK, GM_GROUPS), const),
            _resident((None, D_INNER, d), whole),
            _resident((None, GM_WIDTH, d), whole),
            _resident((None, d, d), whole),
        ],
        out_specs=tuple(out_specs),
        compiler_params=_params(("parallel",), 56),
        name="merge",
    )(ya, proj, proj, proj, x, lw["gm_ln_g"], lw["gm_ln_b"], lw["gm_ws"], lw["gm_bs_t"], wa, wb, wo)
    return (outs[0], outs[1]) if keep_v else (outs[0], None)


def _ple_body(x_ref, pe_ref, g_ref, wg_ref, wp_ref, *rest, final):
    o_ref = rest[-1]
    x = x_ref[...]
    h = _rmsnorm(x, g_ref[...]).astype(BF16)
    out = x + _dot(pe_ref[...].astype(BF16), wp_ref[...]) * jax.nn.sigmoid(_dot(h, wg_ref[...]))
    if final:
        out = _rmsnorm(out, rest[0][...])
    o_ref[...] = out


def _ple(x, pe, g, wg, wp, final_g, layer):
    t, d = x.shape
    tm = _row_tile(t, 512)
    final = final_g is not None
    const = lambda i: (0, 0)
    whole = lambda i: (layer, 0, 0)
    in_specs = [
        pl.BlockSpec((tm, d), lambda i: (i, 0)),
        pl.BlockSpec((None, tm, pe.shape[-1]), lambda i: (layer, i, 0)),
        pl.BlockSpec((1, d), const),
        _resident((None, d, d), whole),
        _resident((None, pe.shape[-1], d), whole),
    ]
    args = [x, pe, g, wg, wp]
    if final:
        in_specs.append(pl.BlockSpec((1, d), const))
        args.append(final_g)
    return pl.pallas_call(
        functools.partial(_ple_body, final=final),
        out_shape=jax.ShapeDtypeStruct((t, d), F32),
        grid=(t // tm,),
        in_specs=in_specs,
        out_specs=pl.BlockSpec((tm, d), lambda i: (i, 0)),
        compiler_params=_params(("parallel",), 48),
        name="ple_final" if final else "ple",
    )(*args)


def _stacked_weights(w):
    stacks = {k: w[k].astype(BF16) for k in (
        "ffn1_w_gate", "ffn1_w_up", "ffn1_w_down", "ffn2_w_gate", "ffn2_w_up", "ffn2_w_down",
        "w_branch_ssd", "w_branch_gmlp", "w_out", "ple_w_gate", "ple_w_proj")}
    w_in_t = jnp.swapaxes(w["w_in"], 1, 2).astype(BF16)
    dt_lo = D_INNER + XBC_DIM
    stacks["w_in_t"] = w_in_t
    stacks["w_dt_t"] = jnp.pad(w_in_t[:, dt_lo:dt_lo + SSD_HEADS, :], ((0, 0), (0, LANES - SSD_HEADS), (0, 0)))
    return stacks


def _layer_weights(i, w):
    pad_heads = lambda v: jnp.pad(v[i].reshape(1, SSD_HEADS), ((0, 0), (0, LANES - SSD_HEADS)))
    rowvec = lambda v: v[i].reshape(1, -1)
    head_of_channel = jnp.arange(D_INNER, dtype=jnp.int32) // SSD_HEAD_DIM
    head_rep = (jnp.arange(LANES, dtype=jnp.int32)[:, None] == head_of_channel[None, :]).astype(BF16)
    return {
        "ffn1_norm": rowvec(w["ffn1_norm"]),
        "mix_norm": rowvec(w["mix_norm"]),
        "conv_w": w["conv_w"][i],
        "conv_b": rowvec(w["conv_b"]),
        "dt_bias": pad_heads(w["dt_bias"]),
        "a_log": pad_heads(w["a_log"]),
        "d_rep": jnp.repeat(w["d_skip"][i], SSD_HEAD_DIM).reshape(1, D_INNER),
        "ssd_norm": rowvec(w["ssd_norm"]),
        "head_rep": head_rep,
        "gm_ln_g": rowvec(w["gm_ln_g"]),
        "gm_ln_b": rowvec(w["gm_ln_b"]),
        "gm_ws": w["gm_ws"][i],
        "gm_bs_t": w["gm_bs"][i].T,
        "ffn2_norm": rowvec(w["ffn2_norm"]),
        "ple_norm": rowvec(w["ple_norm"]),
    }


def _trunk(x, pe, conv_state, ssm_state, sw, layers, final_norm, keep_v):
    nseq, seqlen, d = x.shape
    depth = len(layers)
    t = nseq * seqlen
    x = x.reshape(t, d)
    new_conv, new_ssm, new_v = [], [], []
    for i, lw in enumerate(layers):
        x = _ffn(x, lw["ffn1_norm"], sw["ffn1_w_gate"], sw["ffn1_w_up"], sw["ffn1_w_down"], i)
        proj, dt_raw = _inproj(x, lw["mix_norm"], sw["w_in_t"], sw["w_dt_t"], i)
        ya, conv_out, ssm_out = _ssd(proj, dt_raw, conv_state[i], ssm_state, lw, nseq, seqlen, i)
        x, vn = _merge(ya, proj, x, lw, sw["w_branch_ssd"], sw["w_branch_gmlp"], sw["w_out"], i, seqlen, keep_v)
        x = _ffn(x, lw["ffn2_norm"], sw["ffn2_w_gate"], sw["ffn2_w_up"], sw["ffn2_w_down"], i)
        x = _ple(x, pe.reshape(depth, t, -1), lw["ple_norm"], sw["ple_w_gate"], sw["ple_w_proj"],
                 final_norm if i == depth - 1 else None, i)
        new_conv.append(conv_out)
        new_ssm.append(ssm_out)
        if keep_v:
            new_v.append(vn.reshape(nseq, seqlen, GM_WIDTH))
    y = x.reshape(nseq, seqlen, d)
    return y, jnp.stack(new_conv), jnp.stack(new_ssm), (jnp.stack(new_v) if keep_v else None)


def kernel(x_prompt, x_sample, p_prompt, p_sample, state_conv, state_ssm, ffn1_norm, ffn1_w_gate, ffn1_w_up, ffn1_w_down, mix_norm, w_in, conv_w, conv_b, dt_bias, a_log, d_skip, ssd_norm, gm_ln_g, gm_ln_b, gm_ws, gm_bs, w_branch_ssd, w_branch_gmlp, w_out, ffn2_norm, ffn2_w_gate, ffn2_w_up, ffn2_w_down, ple_norm, ple_w_gate, ple_w_proj, final_norm):
    w = {
        "ffn1_norm": ffn1_norm, "ffn1_w_gate": ffn1_w_gate, "ffn1_w_up": ffn1_w_up, "ffn1_w_down": ffn1_w_down,
        "mix_norm": mix_norm, "w_in": w_in, "conv_w": conv_w, "conv_b": conv_b,
        "dt_bias": dt_bias, "a_log": a_log, "d_skip": d_skip, "ssd_norm": ssd_norm,
        "gm_ln_g": gm_ln_g, "gm_ln_b": gm_ln_b, "gm_ws": gm_ws, "gm_bs": gm_bs,
        "w_branch_ssd": w_branch_ssd, "w_branch_gmlp": w_branch_gmlp, "w_out": w_out,
        "ffn2_norm": ffn2_norm, "ffn2_w_gate": ffn2_w_gate, "ffn2_w_up": ffn2_w_up, "ffn2_w_down": ffn2_w_down,
        "ple_norm": ple_norm, "ple_w_gate": ple_w_gate, "ple_w_proj": ple_w_proj,
    }
    depth = w_in.shape[0]
    layers = [_layer_weights(i, w) for i in range(depth)]
    sw = _stacked_weights(w)
    fnorm = final_norm.reshape(1, -1)
    bp = x_prompt.shape[0]
    conv0 = jnp.zeros((depth, bp, CONV_W - 1, XBC_DIM), F32)
    ssm0 = jnp.zeros((depth, bp, SSD_HEADS, SSD_HEAD_DIM, D_STATE), F32)
    y_prompt, conv_prompt, ssm_prompt, _ = _trunk(x_prompt, p_prompt, conv0, ssm0, sw, layers, fnorm, False)
    y_sample, conv_sample, ssm_sample, v_sample = _trunk(x_sample, p_sample, state_conv, state_ssm, sw, layers, fnorm, True)
    return (y_prompt, y_sample, ssm_prompt, conv_prompt, ssm_sample, conv_sample, v_sample)
```

```python
import functools

import jax
import jax.numpy as jnp
from jax import lax
from jax.experimental import pallas as pl
from jax.experimental.pallas import tpu as pltpu

F32 = jnp.float32
BF16 = jnp.bfloat16

D_MODEL = 2048
SSD_HEADS = 32
SSD_HEAD_DIM = 64
D_INNER = SSD_HEADS * SSD_HEAD_DIM
SSD_GROUPS = 4
HEADS_PER_GROUP = SSD_HEADS // SSD_GROUPS
D_STATE = 128
CONV_W = 4
BC_DIM = SSD_GROUPS * D_STATE
XBC_DIM = D_INNER + 2 * BC_DIM
GM_CHUNK = 128
GM_GROUPS = 8
GM_GROUP_DIM = 128
GM_WIDTH = GM_GROUPS * GM_GROUP_DIM
EPS = 1e-6

LANES = 128
SUBLANES = 8
MIB = 1024 * 1024

PROJ_TN = 1024
XBC_OFF = 0
Z_OFF = 4096
UV_OFF = 6144
GATE_OFF = 8192
PROJ_COLS = 12288
PROJ_PAD_TILE = XBC_DIM // PROJ_TN
PROJ_W_COLS = XBC_DIM + D_INNER + 2 * GM_WIDTH + 2 * D_MODEL

SSD_CHUNK = 128
BF16_ROWS = 2 * SUBLANES
CONV_PAD_ROWS = BF16_ROWS


def _params(semantics, vmem_mib):
    return pltpu.CompilerParams(dimension_semantics=semantics, vmem_limit_bytes=vmem_mib * MIB)


def _rmsnorm(x, g):
    return x * lax.rsqrt(jnp.mean(x * x, axis=-1, keepdims=True) + EPS) * g


def _dot(a, b):
    return jnp.dot(a, b, preferred_element_type=F32)


def _row_tile(t, want):
    return want if t % want == 0 else t


def _ffn_body(x_ref, g_ref, wg_ref, wu_ref, wd_ref, o_ref, h_ref):
    j = pl.program_id(1)

    @pl.when(j == 0)
    def _():
        x = x_ref[...]
        h_ref[...] = _rmsnorm(x, g_ref[...]).astype(BF16)
        o_ref[...] = x

    h = h_ref[...]
    act = (0.5 * jax.nn.silu(_dot(h, wg_ref[...])) * _dot(h, wu_ref[...])).astype(BF16)
    o_ref[...] += _dot(act, wd_ref[...])


def _ffn(x, g, wg, wu, wd, layer):
    t, d = x.shape
    dff = wg.shape[-1]
    tm = _row_tile(t, 1024)
    tf = 512
    return pl.pallas_call(
        _ffn_body,
        out_shape=jax.ShapeDtypeStruct((t, d), F32),
        grid=(t // tm, dff // tf),
        in_specs=[
            pl.BlockSpec((tm, d), lambda i, j: (i, 0)),
            pl.BlockSpec((1, d), lambda i, j: (0, 0)),
            pl.BlockSpec((None, d, tf), lambda i, j: (layer, 0, j)),
            pl.BlockSpec((None, d, tf), lambda i, j: (layer, 0, j)),
            pl.BlockSpec((None, tf, d), lambda i, j: (layer, j, 0)),
        ],
        out_specs=pl.BlockSpec((tm, d), lambda i, j: (i, 0)),
        scratch_shapes=[pltpu.VMEM((tm, d), BF16)],
        compiler_params=_params(("parallel", "arbitrary"), 58),
        name="ffn",
    )(x, g, wg, wu, wd)


def _dot_nt(a, b_t):
    return lax.dot_general(a, b_t, (((1,), (1,)), ((), ())), preferred_element_type=F32)


def _inproj_body(x_ref, g_ref, wt_ref, wdt_t_ref, o_ref, dt_ref, h_ref):
    @pl.when(pl.program_id(1) == 0)
    def _():
        h = _rmsnorm(x_ref[...], g_ref[...]).astype(BF16)
        h_ref[...] = h
        dt_ref[...] = _dot_nt(h, wdt_t_ref[...])

    o_ref[...] = _dot_nt(h_ref[...], wt_ref[0]).astype(o_ref.dtype)


def _w_in_row_of_tile(j):
    xbc_tiles = XBC_DIM // PROJ_TN
    z_tiles = D_INNER // PROJ_TN
    after_dt = D_INNER + XBC_DIM + SSD_HEADS
    return jnp.where(j < xbc_tiles, D_INNER + j * PROJ_TN,
                     jnp.where(j < xbc_tiles + z_tiles, (j - xbc_tiles) * PROJ_TN,
                               after_dt + (j - xbc_tiles - z_tiles) * PROJ_TN))


def _inproj(x, g, w_t, wdt_t, layer):
    t, d = x.shape
    tm = _row_tile(t, 1024)
    n_tiles = PROJ_W_COLS // PROJ_TN
    return pl.pallas_call(
        _inproj_body,
        out_shape=(jax.ShapeDtypeStruct((t, PROJ_COLS), BF16), jax.ShapeDtypeStruct((t, LANES), F32)),
        grid=(t // tm, n_tiles),
        in_specs=[
            pl.BlockSpec((tm, d), lambda i, j: (i, 0)),
            pl.BlockSpec((1, d), lambda i, j: (0, 0)),
            pl.BlockSpec((pl.Element(1), pl.Element(PROJ_TN), pl.Element(d)),
                         lambda i, j: (layer, pl.multiple_of(_w_in_row_of_tile(j), BF16_ROWS), 0)),
            pl.BlockSpec((None, LANES, d), lambda i, j: (layer, 0, 0)),
        ],
        out_specs=(
            pl.BlockSpec((tm, PROJ_TN), lambda i, j: (i, jnp.where(j >= PROJ_PAD_TILE, j + 1, j))),
            pl.BlockSpec((tm, LANES), lambda i, j: (i, 0)),
        ),
        scratch_shapes=[pltpu.VMEM((tm, d), BF16)],
        compiler_params=_params(("parallel", "arbitrary"), 52),
        name="inproj",
    )(x, g, w_t, wdt_t)


def _split_bf16(v, n):
    parts = []
    rest = v
    for _ in range(n):
        p = rest.astype(BF16)
        parts.append(p)
        rest = rest - p.astype(F32)
    return parts


def _dot_split(a_f32, b_bf16, n):
    parts = _split_bf16(a_f32, n)
    out = _dot(parts[0], b_bf16)
    for p in parts[1:]:
        out = out + _dot(p, b_bf16)
    return out


def _dot_split_lhs01(lhs01_bf16, rhs_f32, n):
    parts = _split_bf16(rhs_f32, n)
    out = _dot(lhs01_bf16, parts[0])
    for p in parts[1:]:
        out = out + _dot(lhs01_bf16, p)
    return out


def _transpose_rows_to_lanes(x):
    q = x.shape[0]
    if q == LANES:
        return x.T
    pad = jnp.zeros((LANES - q, x.shape[1]), x.dtype)
    return jnp.concatenate([x, pad], axis=0).T[:, :q]


def _shift_matrices(q):
    t = jnp.arange(q, dtype=jnp.int32)[:, None]
    j = jnp.arange(2 * q, dtype=jnp.int32)[None, :]
    return jnp.concatenate([(j == t + (q - (CONV_W - 1) + k)) for k in range(CONV_W - 1)], axis=0).astype(BF16)


def _conv_silu_chunk(cur, prev_ref, shift_ref, cw_ref, cb_ref, q):
    both = jnp.concatenate([prev_ref[...], cur], axis=0)
    shifted = _dot(shift_ref[...], both)
    acc = cb_ref[...]
    for k in range(CONV_W - 1):
        acc = acc + shifted[k * q:(k + 1) * q] * cw_ref[k:k + 1, :]
    acc = acc + cur.astype(F32) * cw_ref[CONV_W - 1:CONV_W, :]
    prev_ref[...] = cur
    return jax.nn.silu(acc)


def _ssd_chunk(xbc_ref, z_ref, dt_ref, cw_ref, cb_ref, dtb_ref, alog_ref, drep_ref, ng_ref, rep_ref, shift_ref,
               tri_ref, ya_ref, prev_ref, st_ref, q):
    pairs_per_group = HEADS_PER_GROUP // 2
    gw = HEADS_PER_GROUP * SSD_HEAD_DIM

    xc = _conv_silu_chunk(xbc_ref[...], prev_ref, shift_ref, cw_ref, cb_ref, q)
    xs = xc[:, :D_INNER]
    xs_b = xs.astype(BF16)

    dt = jax.nn.softplus(dt_ref[...] + dtb_ref[...])
    d_a = dt * (-jnp.exp(alog_ref[...]))
    row = lax.broadcasted_iota(jnp.int32, (q, q), 0)
    col = lax.broadcasted_iota(jnp.int32, (q, q), 1)
    tril = row >= col
    cum = _dot_split_lhs01(tri_ref[...], d_a, 3)
    cum_last = cum[q - 1:q, :]
    ecum = jnp.exp(cum)
    dec_end = jnp.exp(cum_last - cum)
    cdec = jnp.broadcast_to(jnp.exp(cum_last), (BF16_ROWS, LANES))

    rep = rep_ref[...]
    expanded = _dot_split(jnp.concatenate([ecum, dt * dec_end], axis=0), rep, 2)
    ecum_e = expanded[:q]
    dtdec_e = expanded[q:2 * q]
    cdec_e = _dot_split(cdec, rep, 2)[:1]

    xdtd_b = (xs * dtdec_e).astype(BF16)
    src_t = _transpose_rows_to_lanes(cum - jnp.log(dt))
    left = lax.broadcasted_iota(jnp.int32, (q, LANES), 1) < SSD_HEAD_DIM

    def masked_decay(cb, h):
        seg = cum[:, h:h + 1] - src_t[h:h + 1, :]
        return (cb * jnp.exp(jnp.where(tril, seg, -jnp.inf))).astype(BF16)

    ys = []
    for g in range(SSD_GROUPS):
        b_f32 = xc[:, D_INNER + g * D_STATE:D_INNER + (g + 1) * D_STATE]
        b_g = b_f32.astype(BF16)
        c_g = xc[:, D_INNER + BC_DIM + g * D_STATE:D_INNER + BC_DIM + (g + 1) * D_STATE].astype(BF16)
        cb = lax.dot_general(c_g, b_g, (((1,), (1,)), ((), ())), preferred_element_type=F32)
        st_g = st_ref[g]
        y_off = _dot(c_g, st_g.astype(BF16)) * ecum_e[:, g * gw:(g + 1) * gw]
        for kk in range(pairs_per_group):
            pair = g * pairs_per_group + kk
            m0 = masked_decay(cb, 2 * pair)
            m1 = masked_decay(cb, 2 * pair + 1)
            xpair = xs_b[:, pair * LANES:(pair + 1) * LANES]
            if q % LANES == 0:
                x_l = jnp.where(left, xpair, jnp.zeros_like(xpair))
                x_r = jnp.where(left, jnp.zeros_like(xpair), xpair)
                y_diag = _dot(jnp.concatenate([m0, m1], axis=1), jnp.concatenate([x_l, x_r], axis=0))
            else:
                y_diag = jnp.where(left, _dot(m0, xpair), _dot(m1, xpair))
            ys.append(y_diag + y_off[:, kk * LANES:(kk + 1) * LANES])
        b_t = _transpose_rows_to_lanes(b_f32).astype(BF16)
        st_ref[g] = st_g * cdec_e[:, g * gw:(g + 1) * gw] + _dot(b_t, xdtd_b[:, g * gw:(g + 1) * gw])

    y = jnp.concatenate(ys, axis=1) + xs * drep_ref[...]

    yg = y * jax.nn.silu(z_ref[...].astype(F32))
    normed = []
    for g in range(SSD_GROUPS):
        blk = yg[:, g * gw:(g + 1) * gw]
        normed.append(blk * lax.rsqrt(jnp.mean(blk * blk, axis=-1, keepdims=True) + EPS))
    ya_ref[...] = (jnp.concatenate(normed, axis=1) * ng_ref[...]).astype(BF16)

def _ssd_body(xbc_ref, z_ref, dt_ref, conv0_ref, ssm0_ref, cw_ref, cb_ref, dtb_ref, alog_ref, drep_ref,
              ng_ref, rep_ref, shift_ref, tri_ref, ya_ref, convo_ref, ssmo_ref, prev_ref, tail_ref, st_ref,
              *, q, nc, spb):
    c = pl.program_id(1)
    pairs_per_group = HEADS_PER_GROUP // 2

    @pl.when(c == 0)
    def _():
        for s in range(spb):
            prev_ref[s] = jnp.zeros(prev_ref.shape[1:], BF16)
            prev_ref[s, q - CONV_PAD_ROWS:q, :] = conv0_ref[s].astype(BF16)
            for k in range(SSD_HEADS // 2):
                g, kk = divmod(k, pairs_per_group)
                st_ref[s, g, :, kk * LANES:(kk + 1) * LANES] = ssm0_ref[s, k].T

    for s in range(spb):
        _ssd_chunk(xbc_ref.at[s], z_ref.at[s], dt_ref.at[s], cw_ref, cb_ref, dtb_ref, alog_ref, drep_ref, ng_ref,
                   rep_ref, shift_ref, tri_ref, ya_ref.at[s], prev_ref.at[s], st_ref.at[s], q)

    @pl.when(c == nc - 1)
    def _():
        for s in range(spb):
            tail_ref[s] = xbc_ref[s, q - CONV_PAD_ROWS:q, :].astype(F32)
            convo_ref[s] = tail_ref[s, CONV_PAD_ROWS - (CONV_W - 1):CONV_PAD_ROWS, :]
            for k in range(SSD_HEADS // 2):
                g, kk = divmod(k, pairs_per_group)
                ssmo_ref[s, k] = st_ref[s, g, :, kk * LANES:(kk + 1) * LANES].T


def _ssd(proj, dt_raw, conv0, ssm0_all, lw, nseq, seqlen, layer):
    t = proj.shape[0]
    q = min(SSD_CHUNK, seqlen)
    nc = seqlen // q
    spb = next(n for n in (4, 2, 1) if nseq % n == 0)
    state_block = (spb, SSD_HEADS // 2, 2 * SSD_HEAD_DIM, D_STATE)
    ssm0_pairs = ssm0_all.reshape(ssm0_all.shape[0], nseq, *state_block[1:])
    conv0 = jnp.pad(conv0, ((0, 0), (CONV_PAD_ROWS - (CONV_W - 1), 0), (0, 0)))
    proj3 = proj.reshape(nseq, seqlen, PROJ_COLS)
    const = lambda b, c: (0, 0)
    rowblk = lambda b, c: (b, c, 0)
    ya, conv_out, ssm_out = pl.pallas_call(
        functools.partial(_ssd_body, q=q, nc=nc, spb=spb),
        out_shape=(
            jax.ShapeDtypeStruct((nseq, seqlen, D_INNER), BF16),
            jax.ShapeDtypeStruct((nseq, CONV_W - 1, XBC_DIM), F32),
            jax.ShapeDtypeStruct((nseq,) + state_block[1:], F32),
        ),
        grid=(nseq // spb, nc),
        in_specs=[
            pl.BlockSpec((spb, q, XBC_DIM), rowblk),
            pl.BlockSpec((spb, q, D_INNER), lambda b, c: (b, c, Z_OFF // D_INNER)),
            pl.BlockSpec((spb, q, LANES), rowblk),
            pl.BlockSpec((spb, CONV_PAD_ROWS, XBC_DIM), lambda b, c: (b, 0, 0)),
            pl.BlockSpec((None,) + state_block, lambda b, c: (layer, b, 0, 0, 0)),
            pl.BlockSpec((CONV_W, XBC_DIM), const),
            pl.BlockSpec((1, XBC_DIM), const),
            pl.BlockSpec((1, LANES), const),
            pl.BlockSpec((1, LANES), const),
            pl.BlockSpec((1, D_INNER), const),
            pl.BlockSpec((1, D_INNER), const),
            pl.BlockSpec((LANES, D_INNER), const),
            pl.BlockSpec(((CONV_W - 1) * q, 2 * q), const),
            pl.BlockSpec((q, q), const),
        ],
        out_specs=(
            pl.BlockSpec((spb, q, D_INNER), rowblk),
            pl.BlockSpec((spb, CONV_W - 1, XBC_DIM), lambda b, c: (b, 0, 0)),
            pl.BlockSpec(state_block, lambda b, c: (b, 0, 0, 0)),
        ),
        scratch_shapes=[
            pltpu.VMEM((spb, q, XBC_DIM), BF16),
            pltpu.VMEM((spb, CONV_PAD_ROWS, XBC_DIM), F32),
            pltpu.VMEM((spb, SSD_GROUPS, D_STATE, HEADS_PER_GROUP * SSD_HEAD_DIM), F32),
        ],
        compiler_params=_params(("parallel", "arbitrary"), 48),
        name="ssd",
    )(proj3, proj3, dt_raw.reshape(nseq, seqlen, LANES), conv0, ssm0_pairs, lw["conv_w"], lw["conv_b"],
      lw["dt_bias"], lw["a_log"], lw["d_rep"], lw["ssd_norm"], lw["head_rep"], _shift_matrices(q),
      jnp.tri(q, dtype=BF16))
    return ya.reshape(t, D_INNER), conv_out, ssm_out.reshape(ssm0_all.shape[1:])


def _gmlp_rows(uv_ref, lg_ref, lb_ref, ws_ref, bst_ref, vn_ref, tc, nchunks):
    uv = jax.nn.gelu(uv_ref[...].astype(F32))
    u = uv[:, :GM_WIDTH]
    v = uv[:, GM_WIDTH:]
    mu = jnp.mean(v, axis=-1, keepdims=True)
    dev = v - mu
    var = jnp.mean(dev * dev, axis=-1, keepdims=True)
    vn = dev * lax.rsqrt(var + EPS) * lg_ref[...] + lb_ref[...]
    if vn_ref is not None:
        vn_ref[...] = vn
    vb = vn.astype(BF16)

    row = lax.broadcasted_iota(jnp.int32, (tc, tc), 0)
    col = lax.broadcasted_iota(jnp.int32, (tc, tc), 1)
    tril = row >= col
    mixed = [[None] * GM_GROUPS for _ in range(nchunks)]
    for g in range(GM_GROUPS):
        w = jnp.where(tril, ws_ref[g, :tc, :tc], 0.0).astype(BF16)
        bias = bst_ref[:tc, g:g + 1]
        cols = slice(g * GM_GROUP_DIM, (g + 1) * GM_GROUP_DIM)
        if tc == LANES:
            rhs = jnp.concatenate([vb[r * tc:(r + 1) * tc, cols] for r in range(nchunks)], axis=1)
            out = _dot(w, rhs) + bias
            for r in range(nchunks):
                mixed[r][g] = out[:, r * GM_GROUP_DIM:(r + 1) * GM_GROUP_DIM]
        else:
            for r in range(nchunks):
                mixed[r][g] = _dot(w, vb[r * tc:(r + 1) * tc, cols]) + bias
    full = jnp.concatenate([jnp.concatenate(m, axis=1) for m in mixed], axis=0)
    return (u * full).astype(BF16)


def _merge_body(ya_ref, uv_ref, ga_ref, gb_ref, x_ref, lg_ref, lb_ref, ws_ref, bst_ref, wa_ref, wb_ref, wo_ref,
                o_ref, *maybe_vn_ref, tc, nchunks):
    pa = _dot(ya_ref[...], wa_ref[...])
    yb = _gmlp_rows(uv_ref, lg_ref, lb_ref, ws_ref, bst_ref, maybe_vn_ref[0] if maybe_vn_ref else None, tc, nchunks)
    merged = (jax.nn.sigmoid(ga_ref[...].astype(F32)) * pa
              + jax.nn.sigmoid(gb_ref[...].astype(F32)) * _dot(yb, wb_ref[...]))
    o_ref[...] = x_ref[...] + _dot(merged.astype(BF16), wo_ref[...])


def _resident(block_shape, index_map):
    return pl.BlockSpec(block_shape, index_map, pipeline_mode=pl.Buffered(1))


def _merge(ya, proj, x, lw, wa, wb, wo, layer, seqlen, keep_v):
    t, d = x.shape
    tm = _row_tile(t, 256)
    tc = min(GM_CHUNK, seqlen)
    ga = GATE_OFF // d
    whole = lambda i: (layer, 0, 0)
    const = lambda i: (0, 0)
    out_shape = [jax.ShapeDtypeStruct((t, d), F32)]
    out_specs = [pl.BlockSpec((tm, d), lambda i: (i, 0))]
    if keep_v:
        out_shape.append(jax.ShapeDtypeStruct((t, GM_WIDTH), F32))
        out_specs.append(pl.BlockSpec((tm, GM_WIDTH), lambda i: (i, 0)))
    outs = pl.pallas_call(
        functools.partial(_merge_body, tc=tc, nchunks=tm // tc),
        out_shape=tuple(out_shape),
        grid=(t // tm,),
        in_specs=[
            pl.BlockSpec((tm, D_INNER), lambda i: (i, 0)),
            pl.BlockSpec((tm, 2 * GM_WIDTH), lambda i: (i, UV_OFF // (2 * GM_WIDTH))),
            pl.BlockSpec((tm, d), lambda i: (i, ga)),
            pl.BlockSpec((tm, d), lambda i: (i, ga + 1)),
            pl.BlockSpec((tm, d), lambda i: (i, 0)),
            pl.BlockSpec((1, GM_WIDTH), const),
            pl.BlockSpec((1, GM_WIDTH), const),
            pl.BlockSpec((GM_GROUPS, GM_CHUNK, GM_CHUNK), lambda i: (0, 0, 0)),
            pl.BlockSpec((GM_CHUNK, GM_GROUPS), const),
            _resident((None, D_INNER, d), whole),
            _resident((None, GM_WIDTH, d), whole),
            _resident((None, d, d), whole),
        ],
        out_specs=tuple(out_specs),
        compiler_params=_params(("parallel",), 56),
        name="merge",
    )(ya, proj, proj, proj, x, lw["gm_ln_g"], lw["gm_ln_b"], lw["gm_ws"], lw["gm_bs_t"], wa, wb, wo)
    return (outs[0], outs[1]) if keep_v else (outs[0], None)


def _ple_body(x_ref, pe_ref, g_ref, wg_ref, wp_ref, *rest, final):
    o_ref = rest[-1]
    x = x_ref[...]
    h = _rmsnorm(x, g_ref[...]).astype(BF16)
    out = x + _dot(pe_ref[...].astype(BF16), wp_ref[...]) * jax.nn.sigmoid(_dot(h, wg_ref[...]))
    if final:
        out = _rmsnorm(out, rest[0][...])
    o_ref[...] = out


def _ple(x, pe, g, wg, wp, final_g, layer):
    t, d = x.shape
    tm = _row_tile(t, 512)
    final = final_g is not None
    const = lambda i: (0, 0)
    whole = lambda i: (layer, 0, 0)
    in_specs = [
        pl.BlockSpec((tm, d), lambda i: (i, 0)),
        pl.BlockSpec((None, tm, pe.shape[-1]), lambda i: (layer, i, 0)),
        pl.BlockSpec((1, d), const),
        _resident((None, d, d), whole),
        _resident((None, pe.shape[-1], d), whole),
    ]
    args = [x, pe, g, wg, wp]
    if final:
        in_specs.append(pl.BlockSpec((1, d), const))
        args.append(final_g)
    return pl.pallas_call(
        functools.partial(_ple_body, final=final),
        out_shape=jax.ShapeDtypeStruct((t, d), F32),
        grid=(t // tm,),
        in_specs=in_specs,
        out_specs=pl.BlockSpec((tm, d), lambda i: (i, 0)),
        compiler_params=_params(("parallel",), 48),
        name="ple_final" if final else "ple",
    )(*args)


def _stacked_weights(w):
    stacks = {k: w[k].astype(BF16) for k in (
        "ffn1_w_gate", "ffn1_w_up", "ffn1_w_down", "ffn2_w_gate", "ffn2_w_up", "ffn2_w_down",
        "w_branch_ssd", "w_branch_gmlp", "w_out", "ple_w_gate", "ple_w_proj")}
    w_in_t = jnp.swapaxes(w["w_in"], 1, 2).astype(BF16)
    dt_lo = D_INNER + XBC_DIM
    stacks["w_in_t"] = w_in_t
    stacks["w_dt_t"] = jnp.pad(w_in_t[:, dt_lo:dt_lo + SSD_HEADS, :], ((0, 0), (0, LANES - SSD_HEADS), (0, 0)))
    return stacks


def _layer_weights(i, w):
    pad_heads = lambda v: jnp.pad(v[i].reshape(1, SSD_HEADS), ((0, 0), (0, LANES - SSD_HEADS)))
    rowvec = lambda v: v[i].reshape(1, -1)
    head_of_channel = jnp.arange(D_INNER, dtype=jnp.int32) // SSD_HEAD_DIM
    head_rep = (jnp.arange(LANES, dtype=jnp.int32)[:, None] == head_of_channel[None, :]).astype(BF16)
    return {
        "ffn1_norm": rowvec(w["ffn1_norm"]),
        "mix_norm": rowvec(w["mix_norm"]),
        "conv_w": w["conv_w"][i],
        "conv_b": rowvec(w["conv_b"]),
        "dt_bias": pad_heads(w["dt_bias"]),
        "a_log": pad_heads(w["a_log"]),
        "d_rep": jnp.repeat(w["d_skip"][i], SSD_HEAD_DIM).reshape(1, D_INNER),
        "ssd_norm": rowvec(w["ssd_norm"]),
        "head_rep": head_rep,
        "gm_ln_g": rowvec(w["gm_ln_g"]),
        "gm_ln_b": rowvec(w["gm_ln_b"]),
        "gm_ws": w["gm_ws"][i],
        "gm_bs_t": w["gm_bs"][i].T,
        "ffn2_norm": rowvec(w["ffn2_norm"]),
        "ple_norm": rowvec(w["ple_norm"]),
    }


def _trunk(x, pe, conv_state, ssm_state, sw, layers, final_norm, keep_v):
    nseq, seqlen, d = x.shape
    depth = len(layers)
    t = nseq * seqlen
    x = x.reshape(t, d)
    new_conv, new_ssm, new_v = [], [], []
    for i, lw in enumerate(layers):
        x = _ffn(x, lw["ffn1_norm"], sw["ffn1_w_gate"], sw["ffn1_w_up"], sw["ffn1_w_down"], i)
        proj, dt_raw = _inproj(x, lw["mix_norm"], sw["w_in_t"], sw["w_dt_t"], i)
        ya, conv_out, ssm_out = _ssd(proj, dt_raw, conv_state[i], ssm_state, lw, nseq, seqlen, i)
        x, vn = _merge(ya, proj, x, lw, sw["w_branch_ssd"], sw["w_branch_gmlp"], sw["w_out"], i, seqlen, keep_v)
        x = _ffn(x, lw["ffn2_norm"], sw["ffn2_w_gate"], sw["ffn2_w_up"], sw["ffn2_w_down"], i)
        x = _ple(x, pe.reshape(depth, t, -1), lw["ple_norm"], sw["ple_w_gate"], sw["ple_w_proj"],
                 final_norm if i == depth - 1 else None, i)
        new_conv.append(conv_out)
        new_ssm.append(ssm_out)
        if keep_v:
            new_v.append(vn.reshape(nseq, seqlen, GM_WIDTH))
    y = x.reshape(nseq, seqlen, d)
    return y, jnp.stack(new_conv), jnp.stack(new_ssm), (jnp.stack(new_v) if keep_v else None)


def kernel(x_prompt, x_sample, p_prompt, p_sample, state_conv, state_ssm, ffn1_norm, ffn1_w_gate, ffn1_w_up, ffn1_w_down, mix_norm, w_in, conv_w, conv_b, dt_bias, a_log, d_skip, ssd_norm, gm_ln_g, gm_ln_b, gm_ws, gm_bs, w_branch_ssd, w_branch_gmlp, w_out, ffn2_norm, ffn2_w_gate, ffn2_w_up, ffn2_w_down, ple_norm, ple_w_gate, ple_w_proj, final_norm):
    w = {
        "ffn1_norm": ffn1_norm, "ffn1_w_gate": ffn1_w_gate, "ffn1_w_up": ffn1_w_up, "ffn1_w_down": ffn1_w_down,
        "mix_norm": mix_norm, "w_in": w_in, "conv_w": conv_w, "conv_b": conv_b,
        "dt_bias": dt_bias, "a_log": a_log, "d_skip": d_skip, "ssd_norm": ssd_norm,
        "gm_ln_g": gm_ln_g, "gm_ln_b": gm_ln_b, "gm_ws": gm_ws, "gm_bs": gm_bs,
        "w_branch_ssd": w_branch_ssd, "w_branch_gmlp": w_branch_gmlp, "w_out": w_out,
        "ffn2_norm": ffn2_norm, "ffn2_w_gate": ffn2_w_gate, "ffn2_w_up": ffn2_w_up, "ffn2_w_down": ffn2_w_down,
        "ple_norm": ple_norm, "ple_w_gate": ple_w_gate, "ple_w_proj": ple_w_proj,
    }
    depth = w_in.shape[0]
    layers = [_layer_weights(i, w) for i in range(depth)]
    sw = _stacked_weights(w)
    fnorm = final_norm.reshape(1, -1)
    bp = x_prompt.shape[0]
    conv0 = jnp.zeros((depth, bp, CONV_W - 1, XBC_DIM), F32)
    ssm0 = jnp.zeros((depth, bp, SSD_HEADS, SSD_HEAD_DIM, D_STATE), F32)
    y_prompt, conv_prompt, ssm_prompt, _ = _trunk(x_prompt, p_prompt, conv0, ssm0, sw, layers, fnorm, False)
    y_sample, conv_sample, ssm_sample, v_sample = _trunk(x_sample, p_sample, state_conv, state_ssm, sw, layers, fnorm, True)
    return (y_prompt, y_sample, ssm_prompt, conv_prompt, ssm_sample, conv_sample, v_sample)
```

```python
import functools

import jax
import jax.numpy as jnp
from jax import lax
from jax.experimental import pallas as pl
from jax.experimental.pallas import tpu as pltpu

F32 = jnp.float32
BF16 = jnp.bfloat16

D_MODEL = 2048
SSD_HEADS = 32
SSD_HEAD_DIM = 64
D_INNER = SSD_HEADS * SSD_HEAD_DIM
SSD_GROUPS = 4
HEADS_PER_GROUP = SSD_HEADS // SSD_GROUPS
D_STATE = 128
CONV_W = 4
BC_DIM = SSD_GROUPS * D_STATE
XBC_DIM = D_INNER + 2 * BC_DIM
GM_CHUNK = 128
GM_GROUPS = 8
GM_GROUP_DIM = 128
GM_WIDTH = GM_GROUPS * GM_GROUP_DIM
EPS = 1e-6

LANES = 128
SUBLANES = 8
MIB = 1024 * 1024

PROJ_TN = 1024
XBC_OFF = 0
Z_OFF = 4096
UV_OFF = 6144
GATE_OFF = 8192
PROJ_COLS = 12288
PROJ_PAD_TILE = XBC_DIM // PROJ_TN
PROJ_W_COLS = XBC_DIM + D_INNER + 2 * GM_WIDTH + 2 * D_MODEL

SSD_CHUNK = 128
BF16_ROWS = 2 * SUBLANES
CONV_PAD_ROWS = BF16_ROWS


def _params(semantics, vmem_mib):
    return pltpu.CompilerParams(dimension_semantics=semantics, vmem_limit_bytes=vmem_mib * MIB)


def _rmsnorm(x, g):
    return x * lax.rsqrt(jnp.mean(x * x, axis=-1, keepdims=True) + EPS) * g


def _dot(a, b):
    return jnp.dot(a, b, preferred_element_type=F32)


def _row_tile(t, want):
    return want if t % want == 0 else t


def _ffn_body(x_ref, g_ref, wg_ref, wu_ref, wd_ref, o_ref, h_ref):
    j = pl.program_id(1)

    def half_swiglu(h):
        act = (0.5 * jax.nn.silu(_dot(h, wg_ref[...])) * _dot(h, wu_ref[...])).astype(BF16)
        return _dot(act, wd_ref[...])

    @pl.when(j == 0)
    def _():
        half = x_ref.shape[0] // 2
        for r in range(2):
            rows = slice(r * half, (r + 1) * half)
            x = x_ref[rows, :]
            h = _rmsnorm(x, g_ref[...]).astype(BF16)
            h_ref[rows, :] = h
            o_ref[rows, :] = x + half_swiglu(h)

    @pl.when(j > 0)
    def _():
        o_ref[...] += half_swiglu(h_ref[...])


def _ffn(x, g, wg, wu, wd, layer):
    t, d = x.shape
    dff = wg.shape[-1]
    tm = _row_tile(t, 1024)
    tf = 512
    return pl.pallas_call(
        _ffn_body,
        out_shape=jax.ShapeDtypeStruct((t, d), F32),
        grid=(t // tm, dff // tf),
        in_specs=[
            pl.BlockSpec((tm, d), lambda i, j: (i, 0)),
            pl.BlockSpec((1, d), lambda i, j: (0, 0)),
            pl.BlockSpec((None, d, tf), lambda i, j: (layer, 0, j)),
            pl.BlockSpec((None, d, tf), lambda i, j: (layer, 0, j)),
            pl.BlockSpec((None, tf, d), lambda i, j: (layer, j, 0)),
        ],
        out_specs=pl.BlockSpec((tm, d), lambda i, j: (i, 0)),
        scratch_shapes=[pltpu.VMEM((tm, d), BF16)],
        compiler_params=_params(("parallel", "arbitrary"), 58),
        name="ffn",
    )(x, g, wg, wu, wd)


def _dot_nt(a, b_t):
    return lax.dot_general(a, b_t, (((1,), (1,)), ((), ())), preferred_element_type=F32)


def _inproj_body(x_ref, g_ref, wt_ref, wdt_t_ref, o_ref, dt_ref, h_ref):
    j = pl.program_id(1)

    @pl.when(j == 0)
    def _():
        half = x_ref.shape[0] // 2
        for r in range(2):
            rows = slice(r * half, (r + 1) * half)
            h = _rmsnorm(x_ref[rows, :], g_ref[...]).astype(BF16)
            h_ref[rows, :] = h
            dt_ref[rows, :] = _dot_nt(h, wdt_t_ref[...])
            o_ref[rows, :] = _dot_nt(h, wt_ref[0]).astype(o_ref.dtype)

    @pl.when(j > 0)
    def _():
        o_ref[...] = _dot_nt(h_ref[...], wt_ref[0]).astype(o_ref.dtype)


def _w_in_row_of_tile(j):
    xbc_tiles = XBC_DIM // PROJ_TN
    z_tiles = D_INNER // PROJ_TN
    after_dt = D_INNER + XBC_DIM + SSD_HEADS
    return jnp.where(j < xbc_tiles, D_INNER + j * PROJ_TN,
                     jnp.where(j < xbc_tiles + z_tiles, (j - xbc_tiles) * PROJ_TN,
                               after_dt + (j - xbc_tiles - z_tiles) * PROJ_TN))


def _inproj(x, g, w_t, wdt_t, layer):
    t, d = x.shape
    tm = _row_tile(t, 1024)
    n_tiles = PROJ_W_COLS // PROJ_TN
    return pl.pallas_call(
        _inproj_body,
        out_shape=(jax.ShapeDtypeStruct((t, PROJ_COLS), BF16), jax.ShapeDtypeStruct((t, LANES), F32)),
        grid=(t // tm, n_tiles),
        in_specs=[
            pl.BlockSpec((tm, d), lambda i, j: (i, 0)),
            pl.BlockSpec((1, d), lambda i, j: (0, 0)),
            pl.BlockSpec((pl.Element(1), pl.Element(PROJ_TN), pl.Element(d)),
                         lambda i, j: (layer, pl.multiple_of(_w_in_row_of_tile(j), BF16_ROWS), 0)),
            pl.BlockSpec((None, LANES, d), lambda i, j: (layer, 0, 0)),
        ],
        out_specs=(
            pl.BlockSpec((tm, PROJ_TN), lambda i, j: (i, jnp.where(j >= PROJ_PAD_TILE, j + 1, j))),
            pl.BlockSpec((tm, LANES), lambda i, j: (i, 0)),
        ),
        scratch_shapes=[pltpu.VMEM((tm, d), BF16)],
        compiler_params=_params(("parallel", "arbitrary"), 52),
        name="inproj",
    )(x, g, w_t, wdt_t)


def _split_bf16(v, n):
    parts = []
    rest = v
    for _ in range(n):
        p = rest.astype(BF16)
        parts.append(p)
        rest = rest - p.astype(F32)
    return parts


def _dot_split(a_f32, b_bf16, n):
    parts = _split_bf16(a_f32, n)
    out = _dot(parts[0], b_bf16)
    for p in parts[1:]:
        out = out + _dot(p, b_bf16)
    return out


def _dot_split_lhs01(lhs01_bf16, rhs_f32, n):
    parts = _split_bf16(rhs_f32, n)
    out = _dot(lhs01_bf16, parts[0])
    for p in parts[1:]:
        out = out + _dot(lhs01_bf16, p)
    return out


def _transpose_rows_to_lanes(x):
    q = x.shape[0]
    if q == LANES:
        return x.T
    pad = jnp.zeros((LANES - q, x.shape[1]), x.dtype)
    return jnp.concatenate([x, pad], axis=0).T[:, :q]


def _shift_matrices(q):
    t = jnp.arange(q, dtype=jnp.int32)[:, None]
    j = jnp.arange(2 * q, dtype=jnp.int32)[None, :]
    return jnp.concatenate([(j == t + (q - (CONV_W - 1) + k)) for k in range(CONV_W - 1)], axis=0).astype(BF16)


def _conv_silu_chunk(cur, prev_ref, shift_ref, cw_ref, cb_ref, q):
    both = jnp.concatenate([prev_ref[...], cur], axis=0)
    shifted = _dot(shift_ref[...], both)
    acc = cb_ref[...]
    for k in range(CONV_W - 1):
        acc = acc + shifted[k * q:(k + 1) * q] * cw_ref[k:k + 1, :]
    acc = acc + cur.astype(F32) * cw_ref[CONV_W - 1:CONV_W, :]
    prev_ref[...] = cur
    return jax.nn.silu(acc)


def _ssd_chunk(xbc_ref, z_ref, dt_ref, cw_ref, cb_ref, dtb_ref, alog_ref, drep_ref, ng_ref, rep_ref, shift_ref,
               tri_ref, ya_ref, prev_ref, st_ref, q):
    pairs_per_group = HEADS_PER_GROUP // 2
    gw = HEADS_PER_GROUP * SSD_HEAD_DIM

    xc = _conv_silu_chunk(xbc_ref[...], prev_ref, shift_ref, cw_ref, cb_ref, q)
    xs = xc[:, :D_INNER]
    xs_b = xs.astype(BF16)

    dt = jax.nn.softplus(dt_ref[...] + dtb_ref[...])
    d_a = dt * (-jnp.exp(alog_ref[...]))
    row = lax.broadcasted_iota(jnp.int32, (q, q), 0)
    col = lax.broadcasted_iota(jnp.int32, (q, q), 1)
    tril = row >= col
    cum = _dot_split_lhs01(tri_ref[...], d_a, 3)
    cum_last = cum[q - 1:q, :]
    ecum = jnp.exp(cum)
    dec_end = jnp.exp(cum_last - cum)
    cdec = jnp.broadcast_to(jnp.exp(cum_last), (BF16_ROWS, LANES))

    rep = rep_ref[...]
    expanded = _dot_split(jnp.concatenate([ecum, dt * dec_end], axis=0), rep, 2)
    ecum_e = expanded[:q]
    dtdec_e = expanded[q:2 * q]
    cdec_e = _dot_split(cdec, rep, 2)[:1]

    xdtd_b = (xs * dtdec_e).astype(BF16)
    src_t = _transpose_rows_to_lanes(cum - jnp.log(dt))
    left = lax.broadcasted_iota(jnp.int32, (q, LANES), 1) < SSD_HEAD_DIM

    def masked_decay(cb, h):
        seg = cum[:, h:h + 1] - src_t[h:h + 1, :]
        return (cb * jnp.exp(jnp.where(tril, seg, -jnp.inf))).astype(BF16)

    ys = []
    for g in range(SSD_GROUPS):
        b_f32 = xc[:, D_INNER + g * D_STATE:D_INNER + (g + 1) * D_STATE]
        b_g = b_f32.astype(BF16)
        c_g = xc[:, D_INNER + BC_DIM + g * D_STATE:D_INNER + BC_DIM + (g + 1) * D_STATE].astype(BF16)
        cb = lax.dot_general(c_g, b_g, (((1,), (1,)), ((), ())), preferred_element_type=F32)
        st_g = st_ref[g]
        y_off = _dot(c_g, st_g.astype(BF16)) * ecum_e[:, g * gw:(g + 1) * gw]
        for kk in range(pairs_per_group):
            pair = g * pairs_per_group + kk
            m0 = masked_decay(cb, 2 * pair)
            m1 = masked_decay(cb, 2 * pair + 1)
            xpair = xs_b[:, pair * LANES:(pair + 1) * LANES]
            if q % LANES == 0:
                x_l = jnp.where(left, xpair, jnp.zeros_like(xpair))
                x_r = jnp.where(left, jnp.zeros_like(xpair), xpair)
                y_diag = _dot(jnp.concatenate([m0, m1], axis=1), jnp.concatenate([x_l, x_r], axis=0))
            else:
                y_diag = jnp.where(left, _dot(m0, xpair), _dot(m1, xpair))
            ys.append(y_diag + y_off[:, kk * LANES:(kk + 1) * LANES])
        b_t = _transpose_rows_to_lanes(b_f32).astype(BF16)
        st_ref[g] = st_g * cdec_e[:, g * gw:(g + 1) * gw] + _dot(b_t, xdtd_b[:, g * gw:(g + 1) * gw])

    y = jnp.concatenate(ys, axis=1) + xs * drep_ref[...]

    yg = y * jax.nn.silu(z_ref[...].astype(F32))
    normed = []
    for g in range(SSD_GROUPS):
        blk = yg[:, g * gw:(g + 1) * gw]
        normed.append(blk * lax.rsqrt(jnp.mean(blk * blk, axis=-1, keepdims=True) + EPS))
    ya_ref[...] = (jnp.concatenate(normed, axis=1) * ng_ref[...]).astype(BF16)

def _ssd_body(xbc_ref, z_ref, dt_ref, conv0_ref, ssm0_ref, cw_ref, cb_ref, dtb_ref, alog_ref, drep_ref,
              ng_ref, rep_ref, shift_ref, tri_ref, ya_ref, convo_ref, ssmo_ref, prev_ref, tail_ref, st_ref,
              *, q, nc, spb):
    c = pl.program_id(1)
    pairs_per_group = HEADS_PER_GROUP // 2

    @pl.when(c == 0)
    def _():
        for s in range(spb):
            prev_ref[s] = jnp.zeros(prev_ref.shape[1:], BF16)
            prev_ref[s, q - CONV_PAD_ROWS:q, :] = conv0_ref[s].astype(BF16)
            for k in range(SSD_HEADS // 2):
                g, kk = divmod(k, pairs_per_group)
                st_ref[s, g, :, kk * LANES:(kk + 1) * LANES] = ssm0_ref[s, k].T

    for s in range(spb):
        _ssd_chunk(xbc_ref.at[s], z_ref.at[s], dt_ref.at[s], cw_ref, cb_ref, dtb_ref, alog_ref, drep_ref, ng_ref,
                   rep_ref, shift_ref, tri_ref, ya_ref.at[s], prev_ref.at[s], st_ref.at[s], q)

    @pl.when(c == nc - 1)
    def _():
        for s in range(spb):
            tail_ref[s] = xbc_ref[s, q - CONV_PAD_ROWS:q, :].astype(F32)
            convo_ref[s] = tail_ref[s, CONV_PAD_ROWS - (CONV_W - 1):CONV_PAD_ROWS, :]
            for k in range(SSD_HEADS // 2):
                g, kk = divmod(k, pairs_per_group)
                ssmo_ref[s, k] = st_ref[s, g, :, kk * LANES:(kk + 1) * LANES].T


def _ssd(proj, dt_raw, conv0, ssm0_all, lw, nseq, seqlen, layer):
    t = proj.shape[0]
    q = min(SSD_CHUNK, seqlen)
    nc = seqlen // q
    spb = next(n for n in (4, 2, 1) if nseq % n == 0)
    state_block = (spb, SSD_HEADS // 2, 2 * SSD_HEAD_DIM, D_STATE)
    ssm0_pairs = ssm0_all.reshape(ssm0_all.shape[0], nseq, *state_block[1:])
    conv0 = jnp.pad(conv0, ((0, 0), (CONV_PAD_ROWS - (CONV_W - 1), 0), (0, 0)))
    proj3 = proj.reshape(nseq, seqlen, PROJ_COLS)
    const = lambda b, c: (0, 0)
    rowblk = lambda b, c: (b, c, 0)
    ya, conv_out, ssm_out = pl.pallas_call(
        functools.partial(_ssd_body, q=q, nc=nc, spb=spb),
        out_shape=(
            jax.ShapeDtypeStruct((nseq, seqlen, D_INNER), BF16),
            jax.ShapeDtypeStruct((nseq, CONV_W - 1, XBC_DIM), F32),
            jax.ShapeDtypeStruct((nseq,) + state_block[1:], F32),
        ),
        grid=(nseq // spb, nc),
        in_specs=[
            pl.BlockSpec((spb, q, XBC_DIM), rowblk),
            pl.BlockSpec((spb, q, D_INNER), lambda b, c: (b, c, Z_OFF // D_INNER)),
            pl.BlockSpec((spb, q, LANES), rowblk),
            pl.BlockSpec((spb, CONV_PAD_ROWS, XBC_DIM), lambda b, c: (b, 0, 0)),
            pl.BlockSpec((None,) + state_block, lambda b, c: (layer, b, 0, 0, 0)),
            pl.BlockSpec((CONV_W, XBC_DIM), const),
            pl.BlockSpec((1, XBC_DIM), const),
            pl.BlockSpec((1, LANES), const),
            pl.BlockSpec((1, LANES), const),
            pl.BlockSpec((1, D_INNER), const),
            pl.BlockSpec((1, D_INNER), const),
            pl.BlockSpec((LANES, D_INNER), const),
            pl.BlockSpec(((CONV_W - 1) * q, 2 * q), const),
            pl.BlockSpec((q, q), const),
        ],
        out_specs=(
            pl.BlockSpec((spb, q, D_INNER), rowblk),
            pl.BlockSpec((spb, CONV_W - 1, XBC_DIM), lambda b, c: (b, 0, 0)),
            pl.BlockSpec(state_block, lambda b, c: (b, 0, 0, 0)),
        ),
        scratch_shapes=[
            pltpu.VMEM((spb, q, XBC_DIM), BF16),
            pltpu.VMEM((spb, CONV_PAD_ROWS, XBC_DIM), F32),
            pltpu.VMEM((spb, SSD_GROUPS, D_STATE, HEADS_PER_GROUP * SSD_HEAD_DIM), F32),
        ],
        compiler_params=_params(("parallel", "arbitrary"), 48),
        name="ssd",
    )(proj3, proj3, dt_raw.reshape(nseq, seqlen, LANES), conv0, ssm0_pairs, lw["conv_w"], lw["conv_b"],
      lw["dt_bias"], lw["a_log"], lw["d_rep"], lw["ssd_norm"], lw["head_rep"], _shift_matrices(q),
      jnp.tri(q, dtype=BF16))
    return ya.reshape(t, D_INNER), conv_out, ssm_out.reshape(ssm0_all.shape[1:])


def _gmlp_rows(uv_ref, lg_ref, lb_ref, ws_ref, bst_ref, vn_ref, tc, nchunks):
    uv = jax.nn.gelu(uv_ref[...].astype(F32))
    u = uv[:, :GM_WIDTH]
    v = uv[:, GM_WIDTH:]
    mu = jnp.mean(v, axis=-1, keepdims=True)
    dev = v - mu
    var = jnp.mean(dev * dev, axis=-1, keepdims=True)
    vn = dev * lax.rsqrt(var + EPS) * lg_ref[...] + lb_ref[...]
    if vn_ref is not None:
        vn_ref[...] = vn
    vb = vn.astype(BF16)

    row = lax.broadcasted_iota(jnp.int32, (tc, tc), 0)
    col = lax.broadcasted_iota(jnp.int32, (tc, tc), 1)
    tril = row >= col
    mixed = [[None] * GM_GROUPS for _ in range(nchunks)]
    for g in range(GM_GROUPS):
        w = jnp.where(tril, ws_ref[g, :tc, :tc], 0.0).astype(BF16)
        bias = bst_ref[:tc, g:g + 1]
        cols = slice(g * GM_GROUP_DIM, (g + 1) * GM_GROUP_DIM)
        if tc == LANES:
            rhs = jnp.concatenate([vb[r * tc:(r + 1) * tc, cols] for r in range(nchunks)], axis=1)
            out = _dot(w, rhs) + bias
            for r in range(nchunks):
                mixed[r][g] = out[:, r * GM_GROUP_DIM:(r + 1) * GM_GROUP_DIM]
        else:
            for r in range(nchunks):
                mixed[r][g] = _dot(w, vb[r * tc:(r + 1) * tc, cols]) + bias
    full = jnp.concatenate([jnp.concatenate(m, axis=1) for m in mixed], axis=0)
    return (u * full).astype(BF16)


def _merge_body(ya_ref, uv_ref, ga_ref, gb_ref, x_ref, lg_ref, lb_ref, ws_ref, bst_ref, wa_ref, wb_ref, wo_ref,
                o_ref, *maybe_vn_ref, tc, nchunks):
    pa = _dot(ya_ref[...], wa_ref[...])
    yb = _gmlp_rows(uv_ref, lg_ref, lb_ref, ws_ref, bst_ref, maybe_vn_ref[0] if maybe_vn_ref else None, tc, nchunks)
    merged = (jax.nn.sigmoid(ga_ref[...].astype(F32)) * pa
              + jax.nn.sigmoid(gb_ref[...].astype(F32)) * _dot(yb, wb_ref[...]))
    o_ref[...] = x_ref[...] + _dot(merged.astype(BF16), wo_ref[...])


def _resident(block_shape, index_map):
    return pl.BlockSpec(block_shape, index_map, pipeline_mode=pl.Buffered(1))


def _merge(ya, proj, x, lw, wa, wb, wo, layer, seqlen, keep_v):
    t, d = x.shape
    tm = _row_tile(t, 256)
    tc = min(GM_CHUNK, seqlen)
    ga = GATE_OFF // d
    whole = lambda i: (layer, 0, 0)
    const = lambda i: (0, 0)
    out_shape = [jax.ShapeDtypeStruct((t, d), F32)]
    out_specs = [pl.BlockSpec((tm, d), lambda i: (i, 0))]
    if keep_v:
        out_shape.append(jax.ShapeDtypeStruct((t, GM_WIDTH), F32))
        out_specs.append(pl.BlockSpec((tm, GM_WIDTH), lambda i: (i, 0)))
    outs = pl.pallas_call(
        functools.partial(_merge_body, tc=tc, nchunks=tm // tc),
        out_shape=tuple(out_shape),
        grid=(t // tm,),
        in_specs=[
            pl.BlockSpec((tm, D_INNER), lambda i: (i, 0)),
            pl.BlockSpec((tm, 2 * GM_WIDTH), lambda i: (i, UV_OFF // (2 * GM_WIDTH))),
            pl.BlockSpec((tm, d), lambda i: (i, ga)),
            pl.BlockSpec((tm, d), lambda i: (i, ga + 1)),
            pl.BlockSpec((tm, d), lambda i: (i, 0)),
            pl.BlockSpec((1, GM_WIDTH), const),
            pl.BlockSpec((1, GM_WIDTH), const),
            pl.BlockSpec((GM_GROUPS, GM_CHUNK, GM_CHUNK), lambda i: (0, 0, 0)),
            pl.BlockSpec((GM_CHUNK, GM_GROUPS), const),
            _resident((None, D_INNER, d), whole),
            _resident((None, GM_WIDTH, d), whole),
            _resident((None, d, d), whole),
        ],
        out_specs=tuple(out_specs),
        compiler_params=_params(("parallel",), 56),
        name="merge",
    )(ya, proj, proj, proj, x, lw["gm_ln_g"], lw["gm_ln_b"], lw["gm_ws"], lw["gm_bs_t"], wa, wb, wo)
    return (outs[0], outs[1]) if keep_v else (outs[0], None)


def _ple_body(x_ref, pe_ref, g_ref, wg_ref, wp_ref, *rest, final):
    o_ref = rest[-1]
    half = x_ref.shape[0] // 2
    for r in range(2):
        rows = slice(r * half, (r + 1) * half)
        x = x_ref[rows, :]
        h = _rmsnorm(x, g_ref[...]).astype(BF16)
        out = x + _dot(pe_ref[rows, :].astype(BF16), wp_ref[...]) * jax.nn.sigmoid(_dot(h, wg_ref[...]))
        if final:
            out = _rmsnorm(out, rest[0][...])
        o_ref[rows, :] = out


def _ple(x, pe, g, wg, wp, final_g, layer):
    t, d = x.shape
    tm = _row_tile(t, 512)
    final = final_g is not None
    const = lambda i: (0, 0)
    whole = lambda i: (layer, 0, 0)
    in_specs = [
        pl.BlockSpec((tm, d), lambda i: (i, 0)),
        pl.BlockSpec((None, tm, pe.shape[-1]), lambda i: (layer, i, 0)),
        pl.BlockSpec((1, d), const),
        _resident((None, d, d), whole),
        _resident((None, pe.shape[-1], d), whole),
    ]
    args = [x, pe, g, wg, wp]
    if final:
        in_specs.append(pl.BlockSpec((1, d), const))
        args.append(final_g)
    return pl.pallas_call(
        functools.partial(_ple_body, final=final),
        out_shape=jax.ShapeDtypeStruct((t, d), F32),
        grid=(t // tm,),
        in_specs=in_specs,
        out_specs=pl.BlockSpec((tm, d), lambda i: (i, 0)),
        compiler_params=_params(("parallel",), 48),
        name="ple_final" if final else "ple",
    )(*args)


def _stacked_weights(w):
    stacks = {k: w[k].astype(BF16) for k in (
        "ffn1_w_gate", "ffn1_w_up", "ffn1_w_down", "ffn2_w_gate", "ffn2_w_up", "ffn2_w_down",
        "w_branch_ssd", "w_branch_gmlp", "w_out", "ple_w_gate", "ple_w_proj")}
    w_in_t = jnp.swapaxes(w["w_in"], 1, 2).astype(BF16)
    dt_lo = D_INNER + XBC_DIM
    stacks["w_in_t"] = w_in_t
    stacks["w_dt_t"] = jnp.pad(w_in_t[:, dt_lo:dt_lo + SSD_HEADS, :], ((0, 0), (0, LANES - SSD_HEADS), (0, 0)))
    return stacks


def _layer_weights(i, w):
    pad_heads = lambda v: jnp.pad(v[i].reshape(1, SSD_HEADS), ((0, 0), (0, LANES - SSD_HEADS)))
    rowvec = lambda v: v[i].reshape(1, -1)
    head_of_channel = jnp.arange(D_INNER, dtype=jnp.int32) // SSD_HEAD_DIM
    head_rep = (jnp.arange(LANES, dtype=jnp.int32)[:, None] == head_of_channel[None, :]).astype(BF16)
    return {
        "ffn1_norm": rowvec(w["ffn1_norm"]),
        "mix_norm": rowvec(w["mix_norm"]),
        "conv_w": w["conv_w"][i],
        "conv_b": rowvec(w["conv_b"]),
        "dt_bias": pad_heads(w["dt_bias"]),
        "a_log": pad_heads(w["a_log"]),
        "d_rep": jnp.repeat(w["d_skip"][i], SSD_HEAD_DIM).reshape(1, D_INNER),
        "ssd_norm": rowvec(w["ssd_norm"]),
        "head_rep": head_rep,
        "gm_ln_g": rowvec(w["gm_ln_g"]),
        "gm_ln_b": rowvec(w["gm_ln_b"]),
        "gm_ws": w["gm_ws"][i],
        "gm_bs_t": w["gm_bs"][i].T,
        "ffn2_norm": rowvec(w["ffn2_norm"]),
        "ple_norm": rowvec(w["ple_norm"]),
    }


def _trunk(x, pe, conv_state, ssm_state, sw, layers, final_norm, keep_v):
    nseq, seqlen, d = x.shape
    depth = len(layers)
    t = nseq * seqlen
    x = x.reshape(t, d)
    new_conv, new_ssm, new_v = [], [], []
    for i, lw in enumerate(layers):
        x = _ffn(x, lw["ffn1_norm"], sw["ffn1_w_gate"], sw["ffn1_w_up"], sw["ffn1_w_down"], i)
        proj, dt_raw = _inproj(x, lw["mix_norm"], sw["w_in_t"], sw["w_dt_t"], i)
        ya, conv_out, ssm_out = _ssd(proj, dt_raw, conv_state[i], ssm_state, lw, nseq, seqlen, i)
        x, vn = _merge(ya, proj, x, lw, sw["w_branch_ssd"], sw["w_branch_gmlp"], sw["w_out"], i, seqlen, keep_v)
        x = _ffn(x, lw["ffn2_norm"], sw["ffn2_w_gate"], sw["ffn2_w_up"], sw["ffn2_w_down"], i)
        x = _ple(x, pe.reshape(depth, t, -1), lw["ple_norm"], sw["ple_w_gate"], sw["ple_w_proj"],
                 final_norm if i == depth - 1 else None, i)
        new_conv.append(conv_out)
        new_ssm.append(ssm_out)
        if keep_v:
            new_v.append(vn.reshape(nseq, seqlen, GM_WIDTH))
    y = x.reshape(nseq, seqlen, d)
    return y, jnp.stack(new_conv), jnp.stack(new_ssm), (jnp.stack(new_v) if keep_v else None)


def kernel(x_prompt, x_sample, p_prompt, p_sample, state_conv, state_ssm, ffn1_norm, ffn1_w_gate, ffn1_w_up, ffn1_w_down, mix_norm, w_in, conv_w, conv_b, dt_bias, a_log, d_skip, ssd_norm, gm_ln_g, gm_ln_b, gm_ws, gm_bs, w_branch_ssd, w_branch_gmlp, w_out, ffn2_norm, ffn2_w_gate, ffn2_w_up, ffn2_w_down, ple_norm, ple_w_gate, ple_w_proj, final_norm):
    w = {
        "ffn1_norm": ffn1_norm, "ffn1_w_gate": ffn1_w_gate, "ffn1_w_up": ffn1_w_up, "ffn1_w_down": ffn1_w_down,
        "mix_norm": mix_norm, "w_in": w_in, "conv_w": conv_w, "conv_b": conv_b,
        "dt_bias": dt_bias, "a_log": a_log, "d_skip": d_skip, "ssd_norm": ssd_norm,
        "gm_ln_g": gm_ln_g, "gm_ln_b": gm_ln_b, "gm_ws": gm_ws, "gm_bs": gm_bs,
        "w_branch_ssd": w_branch_ssd, "w_branch_gmlp": w_branch_gmlp, "w_out": w_out,
        "ffn2_norm": ffn2_norm, "ffn2_w_gate": ffn2_w_gate, "ffn2_w_up": ffn2_w_up, "ffn2_w_down": ffn2_w_down,
        "ple_norm": ple_norm, "ple_w_gate": ple_w_gate, "ple_w_proj": ple_w_proj,
    }
    depth = w_in.shape[0]
    layers = [_layer_weights(i, w) for i in range(depth)]
    sw = _stacked_weights(w)
    fnorm = final_norm.reshape(1, -1)
    bp = x_prompt.shape[0]
    conv0 = jnp.zeros((depth, bp, CONV_W - 1, XBC_DIM), F32)
    ssm0 = jnp.zeros((depth, bp, SSD_HEADS, SSD_HEAD_DIM, D_STATE), F32)
    y_prompt, conv_prompt, ssm_prompt, _ = _trunk(x_prompt, p_prompt, conv0, ssm0, sw, layers, fnorm, False)
    y_sample, conv_sample, ssm_sample, v_sample = _trunk(x_sample, p_sample, state_conv, state_ssm, sw, layers, fnorm, True)
    return (y_prompt, y_sample, ssm_prompt, conv_prompt, ssm_sample, conv_sample, v_sample)
```

```python
import functools

import jax
import jax.numpy as jnp
from jax import lax
from jax.experimental import pallas as pl
from jax.experimental.pallas import tpu as pltpu

F32 = jnp.float32
BF16 = jnp.bfloat16

D_MODEL = 2048
SSD_HEADS = 32
SSD_HEAD_DIM = 64
D_INNER = SSD_HEADS * SSD_HEAD_DIM
SSD_GROUPS = 4
HEADS_PER_GROUP = SSD_HEADS // SSD_GROUPS
D_STATE = 128
CONV_W = 4
BC_DIM = SSD_GROUPS * D_STATE
XBC_DIM = D_INNER + 2 * BC_DIM
GM_CHUNK = 128
GM_GROUPS = 8
GM_GROUP_DIM = 128
GM_WIDTH = GM_GROUPS * GM_GROUP_DIM
EPS = 1e-6

LANES = 128
SUBLANES = 8
MIB = 1024 * 1024

PROJ_TN = 1024
XBC_OFF = 0
Z_OFF = XBC_DIM
UV_OFF = 6144
GATE_OFF = 8192
PROJ_COLS = 12288
PROJ_PAD_TILE = (XBC_DIM + D_INNER) // PROJ_TN
PROJ_W_COLS = XBC_DIM + D_INNER + 2 * GM_WIDTH + 2 * D_MODEL

SSD_CHUNK = 128
BF16_ROWS = 2 * SUBLANES
CONV_PAD_ROWS = BF16_ROWS


def _params(semantics, vmem_mib):
    return pltpu.CompilerParams(dimension_semantics=semantics, vmem_limit_bytes=vmem_mib * MIB)


def _rmsnorm(x, g):
    return x * lax.rsqrt(jnp.mean(x * x, axis=-1, keepdims=True) + EPS) * g


def _dot(a, b):
    return jnp.dot(a, b, preferred_element_type=F32)


def _row_tile(t, want):
    return want if t % want == 0 else t


def _ffn_body(x_ref, g_ref, wg_ref, wu_ref, wd_ref, o_ref, h_ref):
    j = pl.program_id(1)

    def half_swiglu(h):
        act = (0.5 * jax.nn.silu(_dot(h, wg_ref[...])) * _dot(h, wu_ref[...])).astype(BF16)
        return _dot(act, wd_ref[...])

    @pl.when(j == 0)
    def _():
        half = x_ref.shape[0] // 2
        for r in range(2):
            rows = slice(r * half, (r + 1) * half)
            x = x_ref[rows, :]
            h = _rmsnorm(x, g_ref[...]).astype(BF16)
            h_ref[rows, :] = h
            o_ref[rows, :] = x + half_swiglu(h)

    @pl.when(j > 0)
    def _():
        o_ref[...] += half_swiglu(h_ref[...])


def _ffn(x, g, wg, wu, wd, layer):
    t, d = x.shape
    dff = wg.shape[-1]
    tm = _row_tile(t, 1024)
    tf = 512
    return pl.pallas_call(
        _ffn_body,
        out_shape=jax.ShapeDtypeStruct((t, d), F32),
        grid=(t // tm, dff // tf),
        in_specs=[
            pl.BlockSpec((tm, d), lambda i, j: (i, 0)),
            pl.BlockSpec((1, d), lambda i, j: (0, 0)),
            pl.BlockSpec((None, d, tf), lambda i, j: (layer, 0, j)),
            pl.BlockSpec((None, d, tf), lambda i, j: (layer, 0, j)),
            pl.BlockSpec((None, tf, d), lambda i, j: (layer, j, 0)),
        ],
        out_specs=pl.BlockSpec((tm, d), lambda i, j: (i, 0)),
        scratch_shapes=[pltpu.VMEM((tm, d), BF16)],
        compiler_params=_params(("parallel", "arbitrary"), 58),
        name="ffn",
    )(x, g, wg, wu, wd)


def _dot_nt(a, b_t):
    return lax.dot_general(a, b_t, (((1,), (1,)), ((), ())), preferred_element_type=F32)


def _inproj_body(x_ref, g_ref, wt_ref, wdt_t_ref, o_ref, dt_ref, h_ref):
    j = pl.program_id(1)

    @pl.when(j == 0)
    def _():
        half = x_ref.shape[0] // 2
        for r in range(2):
            rows = slice(r * half, (r + 1) * half)
            h = _rmsnorm(x_ref[rows, :], g_ref[...]).astype(BF16)
            h_ref[rows, :] = h
            dt_ref[rows, :] = _dot_nt(h, wdt_t_ref[...])
            o_ref[rows, :] = _dot_nt(h, wt_ref[0]).astype(o_ref.dtype)

    @pl.when(j > 0)
    def _():
        o_ref[...] = _dot_nt(h_ref[...], wt_ref[0]).astype(o_ref.dtype)


def _w_in_row_of_tile(j):
    xbc_tiles = XBC_DIM // PROJ_TN
    z_tiles = D_INNER // PROJ_TN
    after_dt = D_INNER + XBC_DIM + SSD_HEADS
    return jnp.where(j < xbc_tiles, D_INNER + j * PROJ_TN,
                     jnp.where(j < xbc_tiles + z_tiles, (j - xbc_tiles) * PROJ_TN,
                               after_dt + (j - xbc_tiles - z_tiles) * PROJ_TN))


def _inproj(x, g, w_t, wdt_t, layer):
    t, d = x.shape
    tm = _row_tile(t, 1024)
    n_tiles = PROJ_W_COLS // PROJ_TN
    return pl.pallas_call(
        _inproj_body,
        out_shape=(jax.ShapeDtypeStruct((t, PROJ_COLS), BF16), jax.ShapeDtypeStruct((t, LANES), F32)),
        grid=(t // tm, n_tiles),
        in_specs=[
            pl.BlockSpec((tm, d), lambda i, j: (i, 0)),
            pl.BlockSpec((1, d), lambda i, j: (0, 0)),
            pl.BlockSpec((pl.Element(1), pl.Element(PROJ_TN), pl.Element(d)),
                         lambda i, j: (layer, pl.multiple_of(_w_in_row_of_tile(j), BF16_ROWS), 0)),
            pl.BlockSpec((None, LANES, d), lambda i, j: (layer, 0, 0)),
        ],
        out_specs=(
            pl.BlockSpec((tm, PROJ_TN), lambda i, j: (i, jnp.where(j >= PROJ_PAD_TILE, j + 1, j))),
            pl.BlockSpec((tm, LANES), lambda i, j: (i, 0)),
        ),
        scratch_shapes=[pltpu.VMEM((tm, d), BF16)],
        compiler_params=_params(("parallel", "arbitrary"), 52),
        name="inproj",
    )(x, g, w_t, wdt_t)


def _split_bf16(v, n):
    parts = []
    rest = v
    for _ in range(n):
        p = rest.astype(BF16)
        parts.append(p)
        rest = rest - p.astype(F32)
    return parts


def _dot_split(a_f32, b_bf16, n):
    parts = _split_bf16(a_f32, n)
    out = _dot(parts[0], b_bf16)
    for p in parts[1:]:
        out = out + _dot(p, b_bf16)
    return out


def _dot_split_lhs01(lhs01_bf16, rhs_f32, n):
    parts = _split_bf16(rhs_f32, n)
    out = _dot(lhs01_bf16, parts[0])
    for p in parts[1:]:
        out = out + _dot(lhs01_bf16, p)
    return out


def _transpose_rows_to_lanes(x):
    q = x.shape[0]
    if q == LANES:
        return x.T
    pad = jnp.zeros((LANES - q, x.shape[1]), x.dtype)
    return jnp.concatenate([x, pad], axis=0).T[:, :q]


def _shift_matrices(q):
    t = jnp.arange(q, dtype=jnp.int32)[:, None]
    j = jnp.arange(2 * q, dtype=jnp.int32)[None, :]
    return jnp.concatenate([(j == t + (q - (CONV_W - 1) + k)) for k in range(CONV_W - 1)], axis=0).astype(BF16)


def _conv_silu_chunk(cur, prev_ref, shift_ref, cw_ref, cb_ref, q):
    both = jnp.concatenate([prev_ref[...], cur], axis=0)
    shifted = _dot(shift_ref[...], both)
    acc = cb_ref[...]
    for k in range(CONV_W - 1):
        acc = acc + shifted[k * q:(k + 1) * q] * cw_ref[k:k + 1, :]
    acc = acc + cur.astype(F32) * cw_ref[CONV_W - 1:CONV_W, :]
    prev_ref[...] = cur
    return jax.nn.silu(acc)


def _ssd_chunk(xbc_ref, z_ref, dt_ref, cw_ref, cb_ref, dtb_ref, alog_ref, drep_ref, ng_ref, rep_ref, shift_ref,
               tri_ref, ya_ref, prev_ref, st_ref, q):
    pairs_per_group = HEADS_PER_GROUP // 2
    gw = HEADS_PER_GROUP * SSD_HEAD_DIM

    xc = _conv_silu_chunk(xbc_ref[...], prev_ref, shift_ref, cw_ref, cb_ref, q)
    xs = xc[:, :D_INNER]
    xs_b = xs.astype(BF16)

    dt = jax.nn.softplus(dt_ref[...] + dtb_ref[...])
    d_a = dt * (-jnp.exp(alog_ref[...]))
    row = lax.broadcasted_iota(jnp.int32, (q, q), 0)
    col = lax.broadcasted_iota(jnp.int32, (q, q), 1)
    tril = row >= col
    cum = _dot_split_lhs01(tri_ref[...], d_a, 3)
    cum_last = cum[q - 1:q, :]
    ecum = jnp.exp(cum)
    dec_end = jnp.exp(cum_last - cum)
    cdec = jnp.broadcast_to(jnp.exp(cum_last), (BF16_ROWS, LANES))

    rep = rep_ref[...]
    expanded = _dot_split(jnp.concatenate([ecum, dt * dec_end], axis=0), rep, 2)
    ecum_e = expanded[:q]
    dtdec_e = expanded[q:2 * q]
    cdec_e = _dot_split(cdec, rep, 2)[:1]

    xdtd_b = (xs * dtdec_e).astype(BF16)
    src_t = _transpose_rows_to_lanes(cum - jnp.log(dt))
    left = lax.broadcasted_iota(jnp.int32, (q, LANES), 1) < SSD_HEAD_DIM

    def masked_decay(cb, h):
        seg = cum[:, h:h + 1] - src_t[h:h + 1, :]
        return (cb * jnp.exp(jnp.where(tril, seg, -jnp.inf))).astype(BF16)

    ys = []
    for g in range(SSD_GROUPS):
        b_f32 = xc[:, D_INNER + g * D_STATE:D_INNER + (g + 1) * D_STATE]
        b_g = b_f32.astype(BF16)
        c_g = xc[:, D_INNER + BC_DIM + g * D_STATE:D_INNER + BC_DIM + (g + 1) * D_STATE].astype(BF16)
        cb = lax.dot_general(c_g, b_g, (((1,), (1,)), ((), ())), preferred_element_type=F32)
        st_g = st_ref[g]
        y_off = _dot(c_g, st_g.astype(BF16)) * ecum_e[:, g * gw:(g + 1) * gw]
        for kk in range(pairs_per_group):
            pair = g * pairs_per_group + kk
            m0 = masked_decay(cb, 2 * pair)
            m1 = masked_decay(cb, 2 * pair + 1)
            xpair = xs_b[:, pair * LANES:(pair + 1) * LANES]
            if q % LANES == 0:
                x_l = jnp.where(left, xpair, jnp.zeros_like(xpair))
                x_r = jnp.where(left, jnp.zeros_like(xpair), xpair)
                y_diag = _dot(jnp.concatenate([m0, m1], axis=1), jnp.concatenate([x_l, x_r], axis=0))
            else:
                y_diag = jnp.where(left, _dot(m0, xpair), _dot(m1, xpair))
            ys.append(y_diag + y_off[:, kk * LANES:(kk + 1) * LANES])
        b_t = _transpose_rows_to_lanes(b_f32).astype(BF16)
        st_ref[g] = st_g * cdec_e[:, g * gw:(g + 1) * gw] + _dot(b_t, xdtd_b[:, g * gw:(g + 1) * gw])

    y = jnp.concatenate(ys, axis=1) + xs * drep_ref[...]

    yg = y * jax.nn.silu(z_ref[...].astype(F32))
    normed = []
    for g in range(SSD_GROUPS):
        blk = yg[:, g * gw:(g + 1) * gw]
        normed.append(blk * lax.rsqrt(jnp.mean(blk * blk, axis=-1, keepdims=True) + EPS))
    ya_ref[...] = (jnp.concatenate(normed, axis=1) * ng_ref[...]).astype(BF16)

def _ssd_body(xz_ref, dt_ref, conv0_ref, ssm0_ref, cw_ref, cb_ref, dtb_ref, alog_ref, drep_ref,
              ng_ref, rep_ref, shift_ref, tri_ref, ya_ref, convo_ref, ssmo_ref, prev_ref, tail_ref, st_ref,
              *, q, nc, spb):
    c = pl.program_id(1)
    pairs_per_group = HEADS_PER_GROUP // 2

    @pl.when(c == 0)
    def _():
        for s in range(spb):
            prev_ref[s] = jnp.zeros(prev_ref.shape[1:], BF16)
            prev_ref[s, q - CONV_PAD_ROWS:q, :] = conv0_ref[s].astype(BF16)
            for k in range(SSD_HEADS // 2):
                g, kk = divmod(k, pairs_per_group)
                st_ref[s, g, :, kk * LANES:(kk + 1) * LANES] = ssm0_ref[s, k].T

    for s in range(spb):
        _ssd_chunk(xz_ref.at[s, :, :XBC_DIM], xz_ref.at[s, :, XBC_DIM:], dt_ref.at[s], cw_ref, cb_ref, dtb_ref, alog_ref, drep_ref, ng_ref,
                   rep_ref, shift_ref, tri_ref, ya_ref.at[s], prev_ref.at[s], st_ref.at[s], q)

    @pl.when(c == nc - 1)
    def _():
        for s in range(spb):
            tail_ref[s] = xz_ref[s, q - CONV_PAD_ROWS:q, :XBC_DIM].astype(F32)
            convo_ref[s] = tail_ref[s, CONV_PAD_ROWS - (CONV_W - 1):CONV_PAD_ROWS, :]
            for k in range(SSD_HEADS // 2):
                g, kk = divmod(k, pairs_per_group)
                ssmo_ref[s, k] = st_ref[s, g, :, kk * LANES:(kk + 1) * LANES].T


def _ssd(proj, dt_raw, conv0, ssm0_all, lw, nseq, seqlen, layer):
    t = proj.shape[0]
    q = min(SSD_CHUNK, seqlen)
    nc = seqlen // q
    spb = next(n for n in (4, 2, 1) if nseq % n == 0)
    state_block = (spb, SSD_HEADS // 2, 2 * SSD_HEAD_DIM, D_STATE)
    ssm0_pairs = ssm0_all.reshape(ssm0_all.shape[0], nseq, *state_block[1:])
    conv0 = jnp.pad(conv0, ((0, 0), (CONV_PAD_ROWS - (CONV_W - 1), 0), (0, 0)))
    proj3 = proj.reshape(nseq, seqlen, PROJ_COLS)
    const = lambda b, c: (0, 0)
    rowblk = lambda b, c: (b, c, 0)
    ya, conv_out, ssm_out = pl.pallas_call(
        functools.partial(_ssd_body, q=q, nc=nc, spb=spb),
        out_shape=(
            jax.ShapeDtypeStruct((nseq, seqlen, D_INNER), BF16),
            jax.ShapeDtypeStruct((nseq, CONV_W - 1, XBC_DIM), F32),
            jax.ShapeDtypeStruct((nseq,) + state_block[1:], F32),
        ),
        grid=(nseq // spb, nc),
        in_specs=[
            pl.BlockSpec((spb, q, XBC_DIM + D_INNER), rowblk),
            pl.BlockSpec((spb, q, LANES), rowblk),
            pl.BlockSpec((spb, CONV_PAD_ROWS, XBC_DIM), lambda b, c: (b, 0, 0)),
            pl.BlockSpec((None,) + state_block, lambda b, c: (layer, b, 0, 0, 0)),
            pl.BlockSpec((CONV_W, XBC_DIM), const),
            pl.BlockSpec((1, XBC_DIM), const),
            pl.BlockSpec((1, LANES), const),
            pl.BlockSpec((1, LANES), const),
            pl.BlockSpec((1, D_INNER), const),
            pl.BlockSpec((1, D_INNER), const),
            pl.BlockSpec((LANES, D_INNER), const),
            pl.BlockSpec(((CONV_W - 1) * q, 2 * q), const),
            pl.BlockSpec((q, q), const),
        ],
        out_specs=(
            pl.BlockSpec((spb, q, D_INNER), rowblk),
            pl.BlockSpec((spb, CONV_W - 1, XBC_DIM), lambda b, c: (b, 0, 0)),
            pl.BlockSpec(state_block, lambda b, c: (b, 0, 0, 0)),
        ),
        scratch_shapes=[
            pltpu.VMEM((spb, q, XBC_DIM), BF16),
            pltpu.VMEM((spb, CONV_PAD_ROWS, XBC_DIM), F32),
            pltpu.VMEM((spb, SSD_GROUPS, D_STATE, HEADS_PER_GROUP * SSD_HEAD_DIM), F32),
        ],
        compiler_params=_params(("parallel", "arbitrary"), 48),
        name="ssd",
    )(proj3, dt_raw.reshape(nseq, seqlen, LANES), conv0, ssm0_pairs, lw["conv_w"], lw["conv_b"],
      lw["dt_bias"], lw["a_log"], lw["d_rep"], lw["ssd_norm"], lw["head_rep"], _shift_matrices(q),
      jnp.tri(q, dtype=BF16))
    return ya.reshape(t, D_INNER), conv_out, ssm_out.reshape(ssm0_all.shape[1:])


def _gmlp_rows(uv_ref, lg_ref, lb_ref, ws_ref, bst_ref, vn_ref, tc, nchunks):
    uv = jax.nn.gelu(uv_ref[...].astype(F32))
    u = uv[:, :GM_WIDTH]
    v = uv[:, GM_WIDTH:]
    mu = jnp.mean(v, axis=-1, keepdims=True)
    dev = v - mu
    var = jnp.mean(dev * dev, axis=-1, keepdims=True)
    vn = dev * lax.rsqrt(var + EPS) * lg_ref[...] + lb_ref[...]
    if vn_ref is not None:
        vn_ref[...] = vn
    vb = vn.astype(BF16)

    row = lax.broadcasted_iota(jnp.int32, (tc, tc), 0)
    col = lax.broadcasted_iota(jnp.int32, (tc, tc), 1)
    tril = row >= col
    mixed = [[None] * GM_GROUPS for _ in range(nchunks)]
    for g in range(GM_GROUPS):
        w = jnp.where(tril, ws_ref[g, :tc, :tc], 0.0).astype(BF16)
        bias = bst_ref[:tc, g:g + 1]
        cols = slice(g * GM_GROUP_DIM, (g + 1) * GM_GROUP_DIM)
        if tc == LANES:
            rhs = jnp.concatenate([vb[r * tc:(r + 1) * tc, cols] for r in range(nchunks)], axis=1)
            out = _dot(w, rhs) + bias
            for r in range(nchunks):
                mixed[r][g] = out[:, r * GM_GROUP_DIM:(r + 1) * GM_GROUP_DIM]
        else:
            for r in range(nchunks):
                mixed[r][g] = _dot(w, vb[r * tc:(r + 1) * tc, cols]) + bias
    full = jnp.concatenate([jnp.concatenate(m, axis=1) for m in mixed], axis=0)
    return (u * full).astype(BF16)


def _merge_body(ya_ref, uvg_ref, x_ref, lg_ref, lb_ref, ws_ref, bst_ref, wa_ref, wb_ref, wo_ref,
                o_ref, *maybe_vn_ref, tc, nchunks):
    d = o_ref.shape[-1]
    uv_ref = uvg_ref.at[:, :2 * GM_WIDTH]
    ga = uvg_ref[:, 2 * GM_WIDTH:2 * GM_WIDTH + d]
    gb = uvg_ref[:, 2 * GM_WIDTH + d:]
    pa = _dot(ya_ref[...], wa_ref[...])
    yb = _gmlp_rows(uv_ref, lg_ref, lb_ref, ws_ref, bst_ref, maybe_vn_ref[0] if maybe_vn_ref else None, tc, nchunks)
    merged = (jax.nn.sigmoid(ga.astype(F32)) * pa + jax.nn.sigmoid(gb.astype(F32)) * _dot(yb, wb_ref[...]))
    o_ref[...] = x_ref[...] + _dot(merged.astype(BF16), wo_ref[...])


def _resident(block_shape, index_map):
    return pl.BlockSpec(block_shape, index_map, pipeline_mode=pl.Buffered(1))


def _merge(ya, proj, x, lw, wa, wb, wo, layer, seqlen, keep_v):
    t, d = x.shape
    tm = _row_tile(t, 256)
    tc = min(GM_CHUNK, seqlen)
    uvg_cols = PROJ_COLS - UV_OFF
    assert UV_OFF % uvg_cols == 0 and GATE_OFF == UV_OFF + 2 * GM_WIDTH
    whole = lambda i: (layer, 0, 0)
    const = lambda i: (0, 0)
    out_shape = [jax.ShapeDtypeStruct((t, d), F32)]
    out_specs = [pl.BlockSpec((tm, d), lambda i: (i, 0))]
    if keep_v:
        out_shape.append(jax.ShapeDtypeStruct((t, GM_WIDTH), F32))
        out_specs.append(pl.BlockSpec((tm, GM_WIDTH), lambda i: (i, 0)))
    outs = pl.pallas_call(
        functools.partial(_merge_body, tc=tc, nchunks=tm // tc),
        out_shape=tuple(out_shape),
        grid=(t // tm,),
        in_specs=[
            pl.BlockSpec((tm, D_INNER), lambda i: (i, 0)),
            pl.BlockSpec((tm, uvg_cols), lambda i: (i, UV_OFF // uvg_cols)),
            pl.BlockSpec((tm, d), lambda i: (i, 0)),
            pl.BlockSpec((1, GM_WIDTH), const),
            pl.BlockSpec((1, GM_WIDTH), const),
            pl.BlockSpec((GM_GROUPS, GM_CHUNK, GM_CHUNK), lambda i: (0, 0, 0)),
            pl.BlockSpec((GM_CHUNK, GM_GROUPS), const),
            _resident((None, D_INNER, d), whole),
            _resident((None, GM_WIDTH, d), whole),
            _resident((None, d, d), whole),
        ],
        out_specs=tuple(out_specs),
        compiler_params=_params(("parallel",), 56),
        name="merge",
    )(ya, proj, x, lw["gm_ln_g"], lw["gm_ln_b"], lw["gm_ws"], lw["gm_bs_t"], wa, wb, wo)
    return (outs[0], outs[1]) if keep_v else (outs[0], None)


def _ple_body(x_ref, pe_ref, g_ref, wg_ref, wp_ref, *rest, final):
    o_ref = rest[-1]
    half = x_ref.shape[0] // 2
    for r in range(2):
        rows = slice(r * half, (r + 1) * half)
        x = x_ref[rows, :]
        h = _rmsnorm(x, g_ref[...]).astype(BF16)
        out = x + _dot(pe_ref[rows, :].astype(BF16), wp_ref[...]) * jax.nn.sigmoid(_dot(h, wg_ref[...]))
        if final:
            out = _rmsnorm(out, rest[0][...])
        o_ref[rows, :] = out


def _ple(x, pe, g, wg, wp, final_g, layer):
    t, d = x.shape
    tm = _row_tile(t, 512)
    final = final_g is not None
    const = lambda i: (0, 0)
    whole = lambda i: (layer, 0, 0)
    in_specs = [
        pl.BlockSpec((tm, d), lambda i: (i, 0)),
        pl.BlockSpec((None, tm, pe.shape[-1]), lambda i: (layer, i, 0)),
        pl.BlockSpec((1, d), const),
        _resident((None, d, d), whole),
        _resident((None, pe.shape[-1], d), whole),
    ]
    args = [x, pe, g, wg, wp]
    if final:
        in_specs.append(pl.BlockSpec((1, d), const))
        args.append(final_g)
    return pl.pallas_call(
        functools.partial(_ple_body, final=final),
        out_shape=jax.ShapeDtypeStruct((t, d), F32),
        grid=(t // tm,),
        in_specs=in_specs,
        out_specs=pl.BlockSpec((tm, d), lambda i: (i, 0)),
        compiler_params=_params(("parallel",), 48),
        name="ple_final" if final else "ple",
    )(*args)


def _stacked_weights(w):
    stacks = {k: w[k].astype(BF16) for k in (
        "ffn1_w_gate", "ffn1_w_up", "ffn1_w_down", "ffn2_w_gate", "ffn2_w_up", "ffn2_w_down",
        "w_branch_ssd", "w_branch_gmlp", "w_out", "ple_w_gate", "ple_w_proj")}
    w_in_t = jnp.swapaxes(w["w_in"], 1, 2).astype(BF16)
    dt_lo = D_INNER + XBC_DIM
    stacks["w_in_t"] = w_in_t
    stacks["w_dt_t"] = jnp.pad(w_in_t[:, dt_lo:dt_lo + SSD_HEADS, :], ((0, 0), (0, LANES - SSD_HEADS), (0, 0)))
    return stacks


def _layer_weights(i, w):
    pad_heads = lambda v: jnp.pad(v[i].reshape(1, SSD_HEADS), ((0, 0), (0, LANES - SSD_HEADS)))
    rowvec = lambda v: v[i].reshape(1, -1)
    head_of_channel = jnp.arange(D_INNER, dtype=jnp.int32) // SSD_HEAD_DIM
    head_rep = (jnp.arange(LANES, dtype=jnp.int32)[:, None] == head_of_channel[None, :]).astype(BF16)
    return {
        "ffn1_norm": rowvec(w["ffn1_norm"]),
        "mix_norm": rowvec(w["mix_norm"]),
        "conv_w": w["conv_w"][i],
        "conv_b": rowvec(w["conv_b"]),
        "dt_bias": pad_heads(w["dt_bias"]),
        "a_log": pad_heads(w["a_log"]),
        "d_rep": jnp.repeat(w["d_skip"][i], SSD_HEAD_DIM).reshape(1, D_INNER),
        "ssd_norm": rowvec(w["ssd_norm"]),
        "head_rep": head_rep,
        "gm_ln_g": rowvec(w["gm_ln_g"]),
        "gm_ln_b": rowvec(w["gm_ln_b"]),
        "gm_ws": w["gm_ws"][i],
        "gm_bs_t": w["gm_bs"][i].T,
        "ffn2_norm": rowvec(w["ffn2_norm"]),
        "ple_norm": rowvec(w["ple_norm"]),
    }


def _trunk(x, pe, conv_state, ssm_state, sw, layers, final_norm, keep_v):
    nseq, seqlen, d = x.shape
    depth = len(layers)
    t = nseq * seqlen
    x = x.reshape(t, d)
    new_conv, new_ssm, new_v = [], [], []
    for i, lw in enumerate(layers):
        x = _ffn(x, lw["ffn1_norm"], sw["ffn1_w_gate"], sw["ffn1_w_up"], sw["ffn1_w_down"], i)
        proj, dt_raw = _inproj(x, lw["mix_norm"], sw["w_in_t"], sw["w_dt_t"], i)
        ya, conv_out, ssm_out = _ssd(proj, dt_raw, conv_state[i], ssm_state, lw, nseq, seqlen, i)
        x, vn = _merge(ya, proj, x, lw, sw["w_branch_ssd"], sw["w_branch_gmlp"], sw["w_out"], i, seqlen, keep_v)
        x = _ffn(x, lw["ffn2_norm"], sw["ffn2_w_gate"], sw["ffn2_w_up"], sw["ffn2_w_down"], i)
        x = _ple(x, pe.reshape(depth, t, -1), lw["ple_norm"], sw["ple_w_gate"], sw["ple_w_proj"],
                 final_norm if i == depth - 1 else None, i)
        new_conv.append(conv_out)
        new_ssm.append(ssm_out)
        if keep_v:
            new_v.append(vn.reshape(nseq, seqlen, GM_WIDTH))
    y = x.reshape(nseq, seqlen, d)
    return y, jnp.stack(new_conv), jnp.stack(new_ssm), (jnp.stack(new_v) if keep_v else None)


def kernel(x_prompt, x_sample, p_prompt, p_sample, state_conv, state_ssm, ffn1_norm, ffn1_w_gate, ffn1_w_up, ffn1_w_down, mix_norm, w_in, conv_w, conv_b, dt_bias, a_log, d_skip, ssd_norm, gm_ln_g, gm_ln_b, gm_ws, gm_bs, w_branch_ssd, w_branch_gmlp, w_out, ffn2_norm, ffn2_w_gate, ffn2_w_up, ffn2_w_down, ple_norm, ple_w_gate, ple_w_proj, final_norm):
    w = {
        "ffn1_norm": ffn1_norm, "ffn1_w_gate": ffn1_w_gate, "ffn1_w_up": ffn1_w_up, "ffn1_w_down": ffn1_w_down,
        "mix_norm": mix_norm, "w_in": w_in, "conv_w": conv_w, "conv_b": conv_b,
        "dt_bias": dt_bias, "a_log": a_log, "d_skip": d_skip, "ssd_norm": ssd_norm,
        "gm_ln_g": gm_ln_g, "gm_ln_b": gm_ln_b, "gm_ws": gm_ws, "gm_bs": gm_bs,
        "w_branch_ssd": w_branch_ssd, "w_branch_gmlp": w_branch_gmlp, "w_out": w_out,
        "ffn2_norm": ffn2_norm, "ffn2_w_gate": ffn2_w_gate, "ffn2_w_up": ffn2_w_up, "ffn2_w_down": ffn2_w_down,
        "ple_norm": ple_norm, "ple_w_gate": ple_w_gate, "ple_w_proj": ple_w_proj,
    }
    depth = w_in.shape[0]
    layers = [_layer_weights(i, w) for i in range(depth)]
    sw = _stacked_weights(w)
    fnorm = final_norm.reshape(1, -1)
    bp = x_prompt.shape[0]
    conv0 = jnp.zeros((depth, bp, CONV_W - 1, XBC_DIM), F32)
    ssm0 = jnp.zeros((depth, bp, SSD_HEADS, SSD_HEAD_DIM, D_STATE), F32)
    y_prompt, conv_prompt, ssm_prompt, _ = _trunk(x_prompt, p_prompt, conv0, ssm0, sw, layers, fnorm, False)
    y_sample, conv_sample, ssm_sample, v_sample = _trunk(x_sample, p_sample, state_conv, state_ssm, sw, layers, fnorm, True)
    return (y_prompt, y_sample, ssm_prompt, conv_prompt, ssm_sample, conv_sample, v_sample)
```

```python
import functools

import jax
import jax.numpy as jnp
from jax import lax
from jax.experimental import pallas as pl
from jax.experimental.pallas import tpu as pltpu

F32 = jnp.float32
BF16 = jnp.bfloat16

D_MODEL = 2048
SSD_HEADS = 32
SSD_HEAD_DIM = 64
D_INNER = SSD_HEADS * SSD_HEAD_DIM
SSD_GROUPS = 4
HEADS_PER_GROUP = SSD_HEADS // SSD_GROUPS
D_STATE = 128
CONV_W = 4
BC_DIM = SSD_GROUPS * D_STATE
XBC_DIM = D_INNER + 2 * BC_DIM
GM_CHUNK = 128
GM_GROUPS = 8
GM_GROUP_DIM = 128
GM_WIDTH = GM_GROUPS * GM_GROUP_DIM
EPS = 1e-6

LANES = 128
SUBLANES = 8
MIB = 1024 * 1024

PROJ_TN = 1024
XBC_OFF = 0
Z_OFF = XBC_DIM
UV_OFF = 6144
GATE_OFF = 8192
PROJ_COLS = 12288
PROJ_PAD_TILE = (XBC_DIM + D_INNER) // PROJ_TN
PROJ_W_COLS = XBC_DIM + D_INNER + 2 * GM_WIDTH + 2 * D_MODEL

SSD_CHUNK = 128
BF16_ROWS = 2 * SUBLANES
CONV_PAD_ROWS = BF16_ROWS


def _params(semantics, vmem_mib):
    return pltpu.CompilerParams(dimension_semantics=semantics, vmem_limit_bytes=vmem_mib * MIB)


def _rmsnorm(x, g):
    return x * lax.rsqrt(jnp.mean(x * x, axis=-1, keepdims=True) + EPS) * g


def _dot(a, b):
    return jnp.dot(a, b, preferred_element_type=F32)


def _row_tile(t, want):
    return want if t % want == 0 else t


def _ffn_body(x_ref, g_ref, wg_ref, wu_ref, wd_ref, o_ref, h_ref):
    j = pl.program_id(1)

    def half_swiglu(h):
        act = (0.5 * jax.nn.silu(_dot(h, wg_ref[...])) * _dot(h, wu_ref[...])).astype(BF16)
        return _dot(act, wd_ref[...])

    @pl.when(j == 0)
    def _():
        half = x_ref.shape[0] // 2
        for r in range(2):
            rows = slice(r * half, (r + 1) * half)
            x = x_ref[rows, :]
            h = _rmsnorm(x, g_ref[...]).astype(BF16)
            h_ref[rows, :] = h
            o_ref[rows, :] = x + half_swiglu(h)

    @pl.when(j > 0)
    def _():
        o_ref[...] += half_swiglu(h_ref[...])


def _ffn(x, g, wg, wu, wd, layer):
    t, d = x.shape
    dff = wg.shape[-1]
    tm = _row_tile(t, 1024)
    tf = 512
    return pl.pallas_call(
        _ffn_body,
        out_shape=jax.ShapeDtypeStruct((t, d), F32),
        grid=(t // tm, dff // tf),
        in_specs=[
            pl.BlockSpec((tm, d), lambda i, j: (i, 0)),
            pl.BlockSpec((1, d), lambda i, j: (0, 0)),
            pl.BlockSpec((None, d, tf), lambda i, j: (layer, 0, j)),
            pl.BlockSpec((None, d, tf), lambda i, j: (layer, 0, j)),
            pl.BlockSpec((None, tf, d), lambda i, j: (layer, j, 0)),
        ],
        out_specs=pl.BlockSpec((tm, d), lambda i, j: (i, 0)),
        scratch_shapes=[pltpu.VMEM((tm, d), BF16)],
        compiler_params=_params(("parallel", "arbitrary"), 58),
        name="ffn",
    )(x, g, wg, wu, wd)


def _dot_nt(a, b_t):
    return lax.dot_general(a, b_t, (((1,), (1,)), ((), ())), preferred_element_type=F32)


def _inproj_body(x_ref, g_ref, wt_ref, wdt_t_ref, o_ref, dt_ref, h_ref):
    j = pl.program_id(1)

    @pl.when(j == 0)
    def _():
        half = x_ref.shape[0] // 2
        for r in range(2):
            rows = slice(r * half, (r + 1) * half)
            h = _rmsnorm(x_ref[rows, :], g_ref[...]).astype(BF16)
            h_ref[rows, :] = h
            dt_ref[rows, :] = _dot_nt(h, wdt_t_ref[...])
            o_ref[rows, :] = _dot_nt(h, wt_ref[0]).astype(o_ref.dtype)

    @pl.when(j > 0)
    def _():
        o_ref[...] = _dot_nt(h_ref[...], wt_ref[0]).astype(o_ref.dtype)


def _w_in_row_of_tile(j):
    xbc_tiles = XBC_DIM // PROJ_TN
    z_tiles = D_INNER // PROJ_TN
    after_dt = D_INNER + XBC_DIM + SSD_HEADS
    return jnp.where(j < xbc_tiles, D_INNER + j * PROJ_TN,
                     jnp.where(j < xbc_tiles + z_tiles, (j - xbc_tiles) * PROJ_TN,
                               after_dt + (j - xbc_tiles - z_tiles) * PROJ_TN))


def _inproj(x, g, w_t, wdt_t, layer):
    t, d = x.shape
    tm = _row_tile(t, 1024)
    n_tiles = PROJ_W_COLS // PROJ_TN
    return pl.pallas_call(
        _inproj_body,
        out_shape=(jax.ShapeDtypeStruct((t, PROJ_COLS), BF16), jax.ShapeDtypeStruct((t, LANES), F32)),
        grid=(t // tm, n_tiles),
        in_specs=[
            pl.BlockSpec((tm, d), lambda i, j: (i, 0)),
            pl.BlockSpec((1, d), lambda i, j: (0, 0)),
            pl.BlockSpec((pl.Element(1), pl.Element(PROJ_TN), pl.Element(d)),
                         lambda i, j: (layer, pl.multiple_of(_w_in_row_of_tile(j), BF16_ROWS), 0)),
            pl.BlockSpec((None, LANES, d), lambda i, j: (layer, 0, 0)),
        ],
        out_specs=(
            pl.BlockSpec((tm, PROJ_TN), lambda i, j: (i, jnp.where(j >= PROJ_PAD_TILE, j + 1, j))),
            pl.BlockSpec((tm, LANES), lambda i, j: (i, 0)),
        ),
        scratch_shapes=[pltpu.VMEM((tm, d), BF16)],
        compiler_params=_params(("parallel", "arbitrary"), 52),
        name="inproj",
    )(x, g, w_t, wdt_t)


def _split_bf16(v, n):
    parts = []
    rest = v
    for _ in range(n):
        p = rest.astype(BF16)
        parts.append(p)
        rest = rest - p.astype(F32)
    return parts


def _dot_split(a_f32, b_bf16, n):
    parts = _split_bf16(a_f32, n)
    out = _dot(parts[0], b_bf16)
    for p in parts[1:]:
        out = out + _dot(p, b_bf16)
    return out


def _dot_split_lhs01(lhs01_bf16, rhs_f32, n):
    parts = _split_bf16(rhs_f32, n)
    out = _dot(lhs01_bf16, parts[0])
    for p in parts[1:]:
        out = out + _dot(lhs01_bf16, p)
    return out


def _transpose_rows_to_lanes(x):
    q = x.shape[0]
    if q == LANES:
        return x.T
    pad = jnp.zeros((LANES - q, x.shape[1]), x.dtype)
    return jnp.concatenate([x, pad], axis=0).T[:, :q]


def _shift_matrices(q):
    t = jnp.arange(q, dtype=jnp.int32)[:, None]
    j = jnp.arange(2 * q, dtype=jnp.int32)[None, :]
    return jnp.concatenate([(j == t + (q - (CONV_W - 1) + k)) for k in range(CONV_W - 1)], axis=0).astype(BF16)


def _conv_silu_chunk(cur, prev_ref, shift_ref, cw_ref, cb_ref, q):
    both = jnp.concatenate([prev_ref[...], cur], axis=0)
    shifted = _dot(shift_ref[...], both)
    acc = cb_ref[...]
    for k in range(CONV_W - 1):
        acc = acc + shifted[k * q:(k + 1) * q] * cw_ref[k:k + 1, :]
    acc = acc + cur.astype(F32) * cw_ref[CONV_W - 1:CONV_W, :]
    prev_ref[...] = cur
    return jax.nn.silu(acc)


def _ssd_chunk(xbc_ref, z_ref, dt_ref, cw_ref, cb_ref, dtb_ref, alog_ref, drep_ref, ng_ref, rep_ref, shift_ref,
               tri_ref, ya_ref, prev_ref, st_ref, q):
    pairs_per_group = HEADS_PER_GROUP // 2
    gw = HEADS_PER_GROUP * SSD_HEAD_DIM

    xc = _conv_silu_chunk(xbc_ref[...], prev_ref, shift_ref, cw_ref, cb_ref, q)
    xs = xc[:, :D_INNER]
    xs_b = xs.astype(BF16)

    dt = jax.nn.softplus(dt_ref[...] + dtb_ref[...])
    d_a = dt * (-jnp.exp(alog_ref[...]))
    row = lax.broadcasted_iota(jnp.int32, (q, q), 0)
    col = lax.broadcasted_iota(jnp.int32, (q, q), 1)
    tril = row >= col
    cum = _dot_split_lhs01(tri_ref[...], d_a, 3)
    cum_last = cum[q - 1:q, :]
    ecum = jnp.exp(cum)
    dec_end = jnp.exp(cum_last - cum)
    cdec = jnp.broadcast_to(jnp.exp(cum_last), (BF16_ROWS, LANES))

    rep = rep_ref[...]
    expanded = _dot_split(jnp.concatenate([ecum, dt * dec_end], axis=0), rep, 2)
    ecum_e = expanded[:q]
    dtdec_e = expanded[q:2 * q]
    cdec_e = _dot_split(cdec, rep, 2)[:1]

    xdtd_b = (xs * dtdec_e).astype(BF16)
    src_t = _transpose_rows_to_lanes(cum - jnp.log(dt))
    left = lax.broadcasted_iota(jnp.int32, (q, LANES), 1) < SSD_HEAD_DIM

    def masked_decay(cb, h):
        seg = cum[:, h:h + 1] - src_t[h:h + 1, :]
        return (cb * jnp.exp(jnp.where(tril, seg, -jnp.inf))).astype(BF16)

    ys = []
    for g in range(SSD_GROUPS):
        b_f32 = xc[:, D_INNER + g * D_STATE:D_INNER + (g + 1) * D_STATE]
        b_g = b_f32.astype(BF16)
        c_g = xc[:, D_INNER + BC_DIM + g * D_STATE:D_INNER + BC_DIM + (g + 1) * D_STATE].astype(BF16)
        cb = lax.dot_general(c_g, b_g, (((1,), (1,)), ((), ())), preferred_element_type=F32)
        st_g = st_ref[g]
        y_off = _dot(c_g, st_g.astype(BF16)) * ecum_e[:, g * gw:(g + 1) * gw]
        b_t = _transpose_rows_to_lanes(b_f32).astype(BF16)
        st_ref[g] = st_g * cdec_e[:, g * gw:(g + 1) * gw] + _dot(b_t, xdtd_b[:, g * gw:(g + 1) * gw])
        for kk in range(pairs_per_group):
            pair = g * pairs_per_group + kk
            m0 = masked_decay(cb, 2 * pair)
            m1 = masked_decay(cb, 2 * pair + 1)
            xpair = xs_b[:, pair * LANES:(pair + 1) * LANES]
            if q % LANES == 0:
                x_l = jnp.where(left, xpair, jnp.zeros_like(xpair))
                x_r = jnp.where(left, jnp.zeros_like(xpair), xpair)
                y_diag = _dot(jnp.concatenate([m0, m1], axis=1), jnp.concatenate([x_l, x_r], axis=0))
            else:
                y_diag = jnp.where(left, _dot(m0, xpair), _dot(m1, xpair))
            ys.append(y_diag + y_off[:, kk * LANES:(kk + 1) * LANES])

    y = jnp.concatenate(ys, axis=1) + xs * drep_ref[...]

    yg = y * jax.nn.silu(z_ref[...].astype(F32))
    normed = []
    for g in range(SSD_GROUPS):
        blk = yg[:, g * gw:(g + 1) * gw]
        normed.append(blk * lax.rsqrt(jnp.mean(blk * blk, axis=-1, keepdims=True) + EPS))
    ya_ref[...] = (jnp.concatenate(normed, axis=1) * ng_ref[...]).astype(BF16)

def _ssd_body(xz_ref, dt_ref, conv0_ref, ssm0_ref, cw_ref, cb_ref, dtb_ref, alog_ref, drep_ref,
              ng_ref, rep_ref, shift_ref, tri_ref, ya_ref, convo_ref, ssmo_ref, prev_ref, tail_ref, st_ref,
              *, q, nc, spb):
    c = pl.program_id(1)
    pairs_per_group = HEADS_PER_GROUP // 2

    @pl.when(c == 0)
    def _():
        for s in range(spb):
            prev_ref[s] = jnp.zeros(prev_ref.shape[1:], BF16)
            prev_ref[s, q - CONV_PAD_ROWS:q, :] = conv0_ref[s].astype(BF16)
            for k in range(SSD_HEADS // 2):
                g, kk = divmod(k, pairs_per_group)
                st_ref[s, g, :, kk * LANES:(kk + 1) * LANES] = ssm0_ref[s, k].T

    for s in range(spb):
        _ssd_chunk(xz_ref.at[s, :, :XBC_DIM], xz_ref.at[s, :, XBC_DIM:], dt_ref.at[s], cw_ref, cb_ref, dtb_ref, alog_ref, drep_ref, ng_ref,
                   rep_ref, shift_ref, tri_ref, ya_ref.at[s], prev_ref.at[s], st_ref.at[s], q)

    @pl.when(c == nc - 1)
    def _():
        for s in range(spb):
            tail_ref[s] = xz_ref[s, q - CONV_PAD_ROWS:q, :XBC_DIM].astype(F32)
            convo_ref[s] = tail_ref[s, CONV_PAD_ROWS - (CONV_W - 1):CONV_PAD_ROWS, :]
            for k in range(SSD_HEADS // 2):
                g, kk = divmod(k, pairs_per_group)
                ssmo_ref[s, k] = st_ref[s, g, :, kk * LANES:(kk + 1) * LANES].T


def _ssd(proj, dt_raw, conv0, ssm0_all, lw, nseq, seqlen, layer):
    t = proj.shape[0]
    q = min(SSD_CHUNK, seqlen)
    nc = seqlen // q
    spb = next(n for n in (4, 2, 1) if nseq % n == 0)
    state_block = (spb, SSD_HEADS // 2, 2 * SSD_HEAD_DIM, D_STATE)
    ssm0_pairs = ssm0_all.reshape(ssm0_all.shape[0], nseq, *state_block[1:])
    conv0 = jnp.pad(conv0, ((0, 0), (CONV_PAD_ROWS - (CONV_W - 1), 0), (0, 0)))
    proj3 = proj.reshape(nseq, seqlen, PROJ_COLS)
    const = lambda b, c: (0, 0)
    rowblk = lambda b, c: (b, c, 0)
    ya, conv_out, ssm_out = pl.pallas_call(
        functools.partial(_ssd_body, q=q, nc=nc, spb=spb),
        out_shape=(
            jax.ShapeDtypeStruct((nseq, seqlen, D_INNER), BF16),
            jax.ShapeDtypeStruct((nseq, CONV_W - 1, XBC_DIM), F32),
            jax.ShapeDtypeStruct((nseq,) + state_block[1:], F32),
        ),
        grid=(nseq // spb, nc),
        in_specs=[
            pl.BlockSpec((spb, q, XBC_DIM + D_INNER), rowblk),
            pl.BlockSpec((spb, q, LANES), rowblk),
            pl.BlockSpec((spb, CONV_PAD_ROWS, XBC_DIM), lambda b, c: (b, 0, 0)),
            pl.BlockSpec((None,) + state_block, lambda b, c: (layer, b, 0, 0, 0)),
            pl.BlockSpec((CONV_W, XBC_DIM), const),
            pl.BlockSpec((1, XBC_DIM), const),
            pl.BlockSpec((1, LANES), const),
            pl.BlockSpec((1, LANES), const),
            pl.BlockSpec((1, D_INNER), const),
            pl.BlockSpec((1, D_INNER), const),
            pl.BlockSpec((LANES, D_INNER), const),
            pl.BlockSpec(((CONV_W - 1) * q, 2 * q), const),
            pl.BlockSpec((q, q), const),
        ],
        out_specs=(
            pl.BlockSpec((spb, q, D_INNER), rowblk),
            pl.BlockSpec((spb, CONV_W - 1, XBC_DIM), lambda b, c: (b, 0, 0)),
            pl.BlockSpec(state_block, lambda b, c: (b, 0, 0, 0)),
        ),
        scratch_shapes=[
            pltpu.VMEM((spb, q, XBC_DIM), BF16),
            pltpu.VMEM((spb, CONV_PAD_ROWS, XBC_DIM), F32),
            pltpu.VMEM((spb, SSD_GROUPS, D_STATE, HEADS_PER_GROUP * SSD_HEAD_DIM), F32),
        ],
        compiler_params=_params(("parallel", "arbitrary"), 48),
        name="ssd",
    )(proj3, dt_raw.reshape(nseq, seqlen, LANES), conv0, ssm0_pairs, lw["conv_w"], lw["conv_b"],
      lw["dt_bias"], lw["a_log"], lw["d_rep"], lw["ssd_norm"], lw["head_rep"], _shift_matrices(q),
      jnp.tri(q, dtype=BF16))
    return ya.reshape(t, D_INNER), conv_out, ssm_out.reshape(ssm0_all.shape[1:])


def _gmlp_rows(uv_ref, lg_ref, lb_ref, ws_ref, bst_ref, vn_ref, tc, nchunks):
    uv = jax.nn.gelu(uv_ref[...].astype(F32))
    u = uv[:, :GM_WIDTH]
    v = uv[:, GM_WIDTH:]
    mu = jnp.mean(v, axis=-1, keepdims=True)
    dev = v - mu
    var = jnp.mean(dev * dev, axis=-1, keepdims=True)
    vn = dev * lax.rsqrt(var + EPS) * lg_ref[...] + lb_ref[...]
    if vn_ref is not None:
        vn_ref[...] = vn
    vb = vn.astype(BF16)

    row = lax.broadcasted_iota(jnp.int32, (tc, tc), 0)
    col = lax.broadcasted_iota(jnp.int32, (tc, tc), 1)
    tril = row >= col
    mixed = [[None] * GM_GROUPS for _ in range(nchunks)]
    for g in range(GM_GROUPS):
        w = jnp.where(tril, ws_ref[g, :tc, :tc], 0.0).astype(BF16)
        bias = bst_ref[:tc, g:g + 1]
        cols = slice(g * GM_GROUP_DIM, (g + 1) * GM_GROUP_DIM)
        if tc == LANES:
            rhs = jnp.concatenate([vb[r * tc:(r + 1) * tc, cols] for r in range(nchunks)], axis=1)
            out = _dot(w, rhs) + bias
            for r in range(nchunks):
                mixed[r][g] = out[:, r * GM_GROUP_DIM:(r + 1) * GM_GROUP_DIM]
        else:
            for r in range(nchunks):
                mixed[r][g] = _dot(w, vb[r * tc:(r + 1) * tc, cols]) + bias
    full = jnp.concatenate([jnp.concatenate(m, axis=1) for m in mixed], axis=0)
    return (u * full).astype(BF16)


def _merge_body(ya_ref, uvg_ref, x_ref, lg_ref, lb_ref, ws_ref, bst_ref, wa_ref, wb_ref, wo_ref,
                o_ref, *maybe_vn_ref, tc, nchunks):
    d = o_ref.shape[-1]
    uv_ref = uvg_ref.at[:, :2 * GM_WIDTH]
    ga = uvg_ref[:, 2 * GM_WIDTH:2 * GM_WIDTH + d]
    gb = uvg_ref[:, 2 * GM_WIDTH + d:]
    pa = _dot(ya_ref[...], wa_ref[...])
    yb = _gmlp_rows(uv_ref, lg_ref, lb_ref, ws_ref, bst_ref, maybe_vn_ref[0] if maybe_vn_ref else None, tc, nchunks)
    merged = (jax.nn.sigmoid(ga.astype(F32)) * pa + jax.nn.sigmoid(gb.astype(F32)) * _dot(yb, wb_ref[...]))
    o_ref[...] = x_ref[...] + _dot(merged.astype(BF16), wo_ref[...])


def _resident(block_shape, index_map):
    return pl.BlockSpec(block_shape, index_map, pipeline_mode=pl.Buffered(1))


def _merge(ya, proj, x, lw, wa, wb, wo, layer, seqlen, keep_v):
    t, d = x.shape
    tm = _row_tile(t, 256)
    tc = min(GM_CHUNK, seqlen)
    uvg_cols = PROJ_COLS - UV_OFF
    assert UV_OFF % uvg_cols == 0 and GATE_OFF == UV_OFF + 2 * GM_WIDTH
    whole = lambda i: (layer, 0, 0)
    const = lambda i: (0, 0)
    out_shape = [jax.ShapeDtypeStruct((t, d), F32)]
    out_specs = [pl.BlockSpec((tm, d), lambda i: (i, 0))]
    if keep_v:
        out_shape.append(jax.ShapeDtypeStruct((t, GM_WIDTH), F32))
        out_specs.append(pl.BlockSpec((tm, GM_WIDTH), lambda i: (i, 0)))
    outs = pl.pallas_call(
        functools.partial(_merge_body, tc=tc, nchunks=tm // tc),
        out_shape=tuple(out_shape),
        grid=(t // tm,),
        in_specs=[
            pl.BlockSpec((tm, D_INNER), lambda i: (i, 0)),
            pl.BlockSpec((tm, uvg_cols), lambda i: (i, UV_OFF // uvg_cols)),
            pl.BlockSpec((tm, d), lambda i: (i, 0)),
            pl.BlockSpec((1, GM_WIDTH), const),
            pl.BlockSpec((1, GM_WIDTH), const),
            pl.BlockSpec((GM_GROUPS, GM_CHUNK, GM_CHUNK), lambda i: (0, 0, 0)),
            pl.BlockSpec((GM_CHUNK, GM_GROUPS), const),
            _resident((None, D_INNER, d), whole),
            _resident((None, GM_WIDTH, d), whole),
            _resident((None, d, d), whole),
        ],
        out_specs=tuple(out_specs),
        compiler_params=_params(("parallel",), 56),
        name="merge",
    )(ya, proj, x, lw["gm_ln_g"], lw["gm_ln_b"], lw["gm_ws"], lw["gm_bs_t"], wa, wb, wo)
    return (outs[0], outs[1]) if keep_v else (outs[0], None)


def _ple_body(x_ref, pe_ref, g_ref, wg_ref, wp_ref, *rest, final):
    o_ref = rest[-1]
    half = x_ref.shape[0] // 2
    for r in range(2):
        rows = slice(r * half, (r + 1) * half)
        x = x_ref[rows, :]
        h = _rmsnorm(x, g_ref[...]).astype(BF16)
        out = x + _dot(pe_ref[rows, :].astype(BF16), wp_ref[...]) * jax.nn.sigmoid(_dot(h, wg_ref[...]))
        if final:
            out = _rmsnorm(out, rest[0][...])
        o_ref[rows, :] = out


def _ple(x, pe, g, wg, wp, final_g, layer):
    t, d = x.shape
    tm = _row_tile(t, 512)
    final = final_g is not None
    const = lambda i: (0, 0)
    whole = lambda i: (layer, 0, 0)
    in_specs = [
        pl.BlockSpec((tm, d), lambda i: (i, 0)),
        pl.BlockSpec((None, tm, pe.shape[-1]), lambda i: (layer, i, 0)),
        pl.BlockSpec((1, d), const),
        _resident((None, d, d), whole),
        _resident((None, pe.shape[-1], d), whole),
    ]
    args = [x, pe, g, wg, wp]
    if final:
        in_specs.append(pl.BlockSpec((1, d), const))
        args.append(final_g)
    return pl.pallas_call(
        functools.partial(_ple_body, final=final),
        out_shape=jax.ShapeDtypeStruct((t, d), F32),
        grid=(t // tm,),
        in_specs=in_specs,
        out_specs=pl.BlockSpec((tm, d), lambda i: (i, 0)),
        compiler_params=_params(("parallel",), 48),
        name="ple_final" if final else "ple",
    )(*args)


def _stacked_weights(w):
    stacks = {k: w[k].astype(BF16) for k in (
        "ffn1_w_gate", "ffn1_w_up", "ffn1_w_down", "ffn2_w_gate", "ffn2_w_up", "ffn2_w_down",
        "w_branch_ssd", "w_branch_gmlp", "w_out", "ple_w_gate", "ple_w_proj")}
    w_in_t = jnp.swapaxes(w["w_in"], 1, 2).astype(BF16)
    dt_lo = D_INNER + XBC_DIM
    stacks["w_in_t"] = w_in_t
    stacks["w_dt_t"] = jnp.pad(w_in_t[:, dt_lo:dt_lo + SSD_HEADS, :], ((0, 0), (0, LANES - SSD_HEADS), (0, 0)))
    return stacks


def _layer_weights(i, w):
    pad_heads = lambda v: jnp.pad(v[i].reshape(1, SSD_HEADS), ((0, 0), (0, LANES - SSD_HEADS)))
    rowvec = lambda v: v[i].reshape(1, -1)
    head_of_channel = jnp.arange(D_INNER, dtype=jnp.int32) // SSD_HEAD_DIM
    head_rep = (jnp.arange(LANES, dtype=jnp.int32)[:, None] == head_of_channel[None, :]).astype(BF16)
    return {
        "ffn1_norm": rowvec(w["ffn1_norm"]),
        "mix_norm": rowvec(w["mix_norm"]),
        "conv_w": w["conv_w"][i],
        "conv_b": rowvec(w["conv_b"]),
        "dt_bias": pad_heads(w["dt_bias"]),
        "a_log": pad_heads(w["a_log"]),
        "d_rep": jnp.repeat(w["d_skip"][i], SSD_HEAD_DIM).reshape(1, D_INNER),
        "ssd_norm": rowvec(w["ssd_norm"]),
        "head_rep": head_rep,
        "gm_ln_g": rowvec(w["gm_ln_g"]),
        "gm_ln_b": rowvec(w["gm_ln_b"]),
        "gm_ws": w["gm_ws"][i],
        "gm_bs_t": w["gm_bs"][i].T,
        "ffn2_norm": rowvec(w["ffn2_norm"]),
        "ple_norm": rowvec(w["ple_norm"]),
    }


def _trunk(x, pe, conv_state, ssm_state, sw, layers, final_norm, keep_v):
    nseq, seqlen, d = x.shape
    depth = len(layers)
    t = nseq * seqlen
    x = x.reshape(t, d)
    new_conv, new_ssm, new_v = [], [], []
    for i, lw in enumerate(layers):
        x = _ffn(x, lw["ffn1_norm"], sw["ffn1_w_gate"], sw["ffn1_w_up"], sw["ffn1_w_down"], i)
        proj, dt_raw = _inproj(x, lw["mix_norm"], sw["w_in_t"], sw["w_dt_t"], i)
        ya, conv_out, ssm_out = _ssd(proj, dt_raw, conv_state[i], ssm_state, lw, nseq, seqlen, i)
        x, vn = _merge(ya, proj, x, lw, sw["w_branch_ssd"], sw["w_branch_gmlp"], sw["w_out"], i, seqlen, keep_v)
        x = _ffn(x, lw["ffn2_norm"], sw["ffn2_w_gate"], sw["ffn2_w_up"], sw["ffn2_w_down"], i)
        x = _ple(x, pe.reshape(depth, t, -1), lw["ple_norm"], sw["ple_w_gate"], sw["ple_w_proj"],
                 final_norm if i == depth - 1 else None, i)
        new_conv.append(conv_out)
        new_ssm.append(ssm_out)
        if keep_v:
            new_v.append(vn.reshape(nseq, seqlen, GM_WIDTH))
    y = x.reshape(nseq, seqlen, d)
    return y, jnp.stack(new_conv), jnp.stack(new_ssm), (jnp.stack(new_v) if keep_v else None)


def kernel(x_prompt, x_sample, p_prompt, p_sample, state_conv, state_ssm, ffn1_norm, ffn1_w_gate, ffn1_w_up, ffn1_w_down, mix_norm, w_in, conv_w, conv_b, dt_bias, a_log, d_skip, ssd_norm, gm_ln_g, gm_ln_b, gm_ws, gm_bs, w_branch_ssd, w_branch_gmlp, w_out, ffn2_norm, ffn2_w_gate, ffn2_w_up, ffn2_w_down, ple_norm, ple_w_gate, ple_w_proj, final_norm):
    w = {
        "ffn1_norm": ffn1_norm, "ffn1_w_gate": ffn1_w_gate, "ffn1_w_up": ffn1_w_up, "ffn1_w_down": ffn1_w_down,
        "mix_norm": mix_norm, "w_in": w_in, "conv_w": conv_w, "conv_b": conv_b,
        "dt_bias": dt_bias, "a_log": a_log, "d_skip": d_skip, "ssd_norm": ssd_norm,
        "gm_ln_g": gm_ln_g, "gm_ln_b": gm_ln_b, "gm_ws": gm_ws, "gm_bs": gm_bs,
        "w_branch_ssd": w_branch_ssd, "w_branch_gmlp": w_branch_gmlp, "w_out": w_out,
        "ffn2_norm": ffn2_norm, "ffn2_w_gate": ffn2_w_gate, "ffn2_w_up": ffn2_w_up, "ffn2_w_down": ffn2_w_down,
        "ple_norm": ple_norm, "ple_w_gate": ple_w_gate, "ple_w_proj": ple_w_proj,
    }
    depth = w_in.shape[0]
    layers = [_layer_weights(i, w) for i in range(depth)]
    sw = _stacked_weights(w)
    fnorm = final_norm.reshape(1, -1)
    bp = x_prompt.shape[0]
    conv0 = jnp.zeros((depth, bp, CONV_W - 1, XBC_DIM), F32)
    ssm0 = jnp.zeros((depth, bp, SSD_HEADS, SSD_HEAD_DIM, D_STATE), F32)
    y_prompt, conv_prompt, ssm_prompt, _ = _trunk(x_prompt, p_prompt, conv0, ssm0, sw, layers, fnorm, False)
    y_sample, conv_sample, ssm_sample, v_sample = _trunk(x_sample, p_sample, state_conv, state_ssm, sw, layers, fnorm, True)
    return (y_prompt, y_sample, ssm_prompt, conv_prompt, ssm_sample, conv_sample, v_sample)
```

```python
import functools

import jax
import jax.numpy as jnp
from jax import lax
from jax.experimental import pallas as pl
from jax.experimental.pallas import tpu as pltpu

F32 = jnp.float32
BF16 = jnp.bfloat16

D_MODEL = 2048
SSD_HEADS = 32
SSD_HEAD_DIM = 64
D_INNER = SSD_HEADS * SSD_HEAD_DIM
SSD_GROUPS = 4
HEADS_PER_GROUP = SSD_HEADS // SSD_GROUPS
D_STATE = 128
CONV_W = 4
BC_DIM = SSD_GROUPS * D_STATE
XBC_DIM = D_INNER + 2 * BC_DIM
GM_CHUNK = 128
GM_GROUPS = 8
GM_GROUP_DIM = 128
GM_WIDTH = GM_GROUPS * GM_GROUP_DIM
EPS = 1e-6

LANES = 128
SUBLANES = 8
MIB = 1024 * 1024

PROJ_TN = 1024
XBC_OFF = 0
Z_OFF = XBC_DIM
UV_OFF = 6144
GATE_OFF = 8192
PROJ_COLS = 12288
PROJ_PAD_TILE = (XBC_DIM + D_INNER) // PROJ_TN
PROJ_W_COLS = XBC_DIM + D_INNER + 2 * GM_WIDTH + 2 * D_MODEL

SSD_CHUNK = 128
BF16_ROWS = 2 * SUBLANES
CONV_PAD_ROWS = BF16_ROWS


def _params(semantics, vmem_mib):
    return pltpu.CompilerParams(dimension_semantics=semantics, vmem_limit_bytes=vmem_mib * MIB)


def _rmsnorm(x, g):
    return x * lax.rsqrt(jnp.mean(x * x, axis=-1, keepdims=True) + EPS) * g


def _dot(a, b):
    return jnp.dot(a, b, preferred_element_type=F32)


def _row_tile(t, want):
    return want if t % want == 0 else t


def _ffn_body(x_ref, g_ref, wg_ref, wu_ref, wd_ref, o_ref, h_ref):
    j = pl.program_id(1)

    def half_swiglu(h):
        act = (0.5 * jax.nn.silu(_dot(h, wg_ref[...])) * _dot(h, wu_ref[...])).astype(BF16)
        return _dot(act, wd_ref[...])

    @pl.when(j == 0)
    def _():
        half = x_ref.shape[0] // 2
        for r in range(2):
            rows = slice(r * half, (r + 1) * half)
            x = x_ref[rows, :]
            h = _rmsnorm(x, g_ref[...]).astype(BF16)
            h_ref[rows, :] = h
            o_ref[rows, :] = x + half_swiglu(h)

    @pl.when(j > 0)
    def _():
        o_ref[...] += half_swiglu(h_ref[...])


def _ffn(x, g, wg, wu, wd, layer):
    t, d = x.shape
    dff = wg.shape[-1]
    tm = _row_tile(t, 1024)
    tf = 512
    return pl.pallas_call(
        _ffn_body,
        out_shape=jax.ShapeDtypeStruct((t, d), F32),
        grid=(t // tm, dff // tf),
        in_specs=[
            pl.BlockSpec((tm, d), lambda i, j: (i, 0)),
            pl.BlockSpec((1, d), lambda i, j: (0, 0)),
            pl.BlockSpec((None, d, tf), lambda i, j: (layer, 0, j)),
            pl.BlockSpec((None, d, tf), lambda i, j: (layer, 0, j)),
            pl.BlockSpec((None, tf, d), lambda i, j: (layer, j, 0)),
        ],
        out_specs=pl.BlockSpec((tm, d), lambda i, j: (i, 0)),
        scratch_shapes=[pltpu.VMEM((tm, d), BF16)],
        compiler_params=_params(("parallel", "arbitrary"), 58),
        name="ffn",
    )(x, g, wg, wu, wd)


def _dot_nt(a, b_t):
    return lax.dot_general(a, b_t, (((1,), (1,)), ((), ())), preferred_element_type=F32)


def _inproj_body(x_ref, g_ref, wt_ref, wdt_t_ref, o_ref, dt_ref, h_ref):
    j = pl.program_id(1)

    @pl.when(j == 0)
    def _():
        half = x_ref.shape[0] // 2
        for r in range(2):
            rows = slice(r * half, (r + 1) * half)
            h = _rmsnorm(x_ref[rows, :], g_ref[...]).astype(BF16)
            h_ref[rows, :] = h
            dt_ref[rows, :] = _dot_nt(h, wdt_t_ref[...])
            o_ref[rows, :] = _dot_nt(h, wt_ref[0]).astype(o_ref.dtype)

    @pl.when(j > 0)
    def _():
        o_ref[...] = _dot_nt(h_ref[...], wt_ref[0]).astype(o_ref.dtype)


def _w_in_row_of_tile(j):
    xbc_tiles = XBC_DIM // PROJ_TN
    z_tiles = D_INNER // PROJ_TN
    after_dt = D_INNER + XBC_DIM + SSD_HEADS
    return jnp.where(j < xbc_tiles, D_INNER + j * PROJ_TN,
                     jnp.where(j < xbc_tiles + z_tiles, (j - xbc_tiles) * PROJ_TN,
                               after_dt + (j - xbc_tiles - z_tiles) * PROJ_TN))


def _inproj(x, g, w_t, wdt_t, layer):
    t, d = x.shape
    tm = _row_tile(t, 1024)
    n_tiles = PROJ_W_COLS // PROJ_TN
    return pl.pallas_call(
        _inproj_body,
        out_shape=(jax.ShapeDtypeStruct((t, PROJ_COLS), BF16), jax.ShapeDtypeStruct((t, LANES), F32)),
        grid=(t // tm, n_tiles),
        in_specs=[
            pl.BlockSpec((tm, d), lambda i, j: (i, 0)),
            pl.BlockSpec((1, d), lambda i, j: (0, 0)),
            pl.BlockSpec((pl.Element(1), pl.Element(PROJ_TN), pl.Element(d)),
                         lambda i, j: (layer, pl.multiple_of(_w_in_row_of_tile(j), BF16_ROWS), 0)),
            pl.BlockSpec((None, LANES, d), lambda i, j: (layer, 0, 0)),
        ],
        out_specs=(
            pl.BlockSpec((tm, PROJ_TN), lambda i, j: (i, jnp.where(j >= PROJ_PAD_TILE, j + 1, j))),
            pl.BlockSpec((tm, LANES), lambda i, j: (i, 0)),
        ),
        scratch_shapes=[pltpu.VMEM((tm, d), BF16)],
        compiler_params=_params(("parallel", "arbitrary"), 52),
        name="inproj",
    )(x, g, w_t, wdt_t)


def _split_bf16(v, n):
    parts = []
    rest = v
    for _ in range(n):
        p = rest.astype(BF16)
        parts.append(p)
        rest = rest - p.astype(F32)
    return parts


def _dot_split(a_f32, b_bf16, n):
    parts = _split_bf16(a_f32, n)
    out = _dot(parts[0], b_bf16)
    for p in parts[1:]:
        out = out + _dot(p, b_bf16)
    return out


def _dot_split_lhs01(lhs01_bf16, rhs_f32, n):
    parts = _split_bf16(rhs_f32, n)
    out = _dot(lhs01_bf16, parts[0])
    for p in parts[1:]:
        out = out + _dot(lhs01_bf16, p)
    return out


def _transpose_rows_to_lanes(x):
    q = x.shape[0]
    if q == LANES:
        return x.T
    pad = jnp.zeros((LANES - q, x.shape[1]), x.dtype)
    return jnp.concatenate([x, pad], axis=0).T[:, :q]


def _shift_matrices(q):
    t = jnp.arange(q, dtype=jnp.int32)[:, None]
    j = jnp.arange(2 * q, dtype=jnp.int32)[None, :]
    return jnp.concatenate([(j == t + (q - (CONV_W - 1) + k)) for k in range(CONV_W - 1)], axis=0).astype(BF16)


def _conv_silu_chunk(cur, prev_ref, shift_ref, cw_ref, cb_ref, q):
    both = jnp.concatenate([prev_ref[...], cur], axis=0)
    shifted = _dot(shift_ref[...], both)
    acc = cb_ref[...]
    for k in range(CONV_W - 1):
        acc = acc + shifted[k * q:(k + 1) * q] * cw_ref[k:k + 1, :]
    acc = acc + cur.astype(F32) * cw_ref[CONV_W - 1:CONV_W, :]
    prev_ref[...] = cur
    return jax.nn.silu(acc)


def _ssd_chunk(xbc_ref, z_ref, dt_ref, cw_ref, cb_ref, dtb_ref, alog_ref, drep_ref, ng_ref, rep_ref, shift_ref,
               tri_ref, ya_ref, prev_ref, st_ref, q):
    pairs_per_group = HEADS_PER_GROUP // 2
    gw = HEADS_PER_GROUP * SSD_HEAD_DIM

    xc = _conv_silu_chunk(xbc_ref[...], prev_ref, shift_ref, cw_ref, cb_ref, q)
    xs = xc[:, :D_INNER]
    xs_b = xs.astype(BF16)

    dt = jax.nn.softplus(dt_ref[...] + dtb_ref[...])
    d_a = dt * (-jnp.exp(alog_ref[...]))
    row = lax.broadcasted_iota(jnp.int32, (q, q), 0)
    col = lax.broadcasted_iota(jnp.int32, (q, q), 1)
    tril = row >= col
    cum = _dot_split_lhs01(tri_ref[...], d_a, 3)
    cum_last = cum[q - 1:q, :]
    ecum = jnp.exp(cum)
    dec_end = jnp.exp(cum_last - cum)
    cdec = jnp.broadcast_to(jnp.exp(cum_last), (BF16_ROWS, LANES))

    rep = rep_ref[...]
    expanded = _dot_split(jnp.concatenate([ecum, dt * dec_end], axis=0), rep, 2)
    ecum_e = expanded[:q]
    dtdec_e = expanded[q:2 * q]
    cdec_e = _dot_split(cdec, rep, 2)[:1]

    xdtd_b = (xs * dtdec_e).astype(BF16)
    src_t = _transpose_rows_to_lanes(cum - jnp.log(dt))
    left = lax.broadcasted_iota(jnp.int32, (q, LANES), 1) < SSD_HEAD_DIM

    def masked_decay(cb, h):
        seg = cum[:, h:h + 1] - src_t[h:h + 1, :]
        return (cb * jnp.exp(jnp.where(tril, seg, -jnp.inf))).astype(BF16)

    ys = []
    for g in range(SSD_GROUPS):
        b_f32 = xc[:, D_INNER + g * D_STATE:D_INNER + (g + 1) * D_STATE]
        b_g = b_f32.astype(BF16)
        c_g = xc[:, D_INNER + BC_DIM + g * D_STATE:D_INNER + BC_DIM + (g + 1) * D_STATE].astype(BF16)
        cb = lax.dot_general(c_g, b_g, (((1,), (1,)), ((), ())), preferred_element_type=F32)
        st_g = st_ref[g]
        y_off = _dot(c_g, st_g.astype(BF16)) * ecum_e[:, g * gw:(g + 1) * gw]
        b_t = _transpose_rows_to_lanes(b_f32).astype(BF16)
        st_ref[g] = st_g * cdec_e[:, g * gw:(g + 1) * gw] + _dot(b_t, xdtd_b[:, g * gw:(g + 1) * gw])
        for kk in range(pairs_per_group):
            pair = g * pairs_per_group + kk
            m0 = masked_decay(cb, 2 * pair)
            m1 = masked_decay(cb, 2 * pair + 1)
            xpair = xs_b[:, pair * LANES:(pair + 1) * LANES]
            if q % LANES == 0:
                x_l = jnp.where(left, xpair, jnp.zeros_like(xpair))
                x_r = jnp.where(left, jnp.zeros_like(xpair), xpair)
                y_diag = _dot(jnp.concatenate([m0, m1], axis=1), jnp.concatenate([x_l, x_r], axis=0))
            else:
                y_diag = jnp.where(left, _dot(m0, xpair), _dot(m1, xpair))
            ys.append(y_diag + y_off[:, kk * LANES:(kk + 1) * LANES])

    y = jnp.concatenate(ys, axis=1) + xs * drep_ref[...]

    yg = y * jax.nn.silu(z_ref[...].astype(F32))
    normed = []
    for g in range(SSD_GROUPS):
        blk = yg[:, g * gw:(g + 1) * gw]
        normed.append(blk * lax.rsqrt(jnp.mean(blk * blk, axis=-1, keepdims=True) + EPS))
    ya_ref[...] = (jnp.concatenate(normed, axis=1) * ng_ref[...]).astype(BF16)

def _ssd_body(xz_ref, dt_ref, conv0_ref, ssm0_ref, cw_ref, cb_ref, dtb_ref, alog_ref, drep_ref,
              ng_ref, rep_ref, shift_ref, tri_ref, ya_ref, convo_ref, ssmo_ref, prev_ref, tail_ref, st_ref,
              *, q, nc, spb):
    c = pl.program_id(1)
    pairs_per_group = HEADS_PER_GROUP // 2

    @pl.when(c == 0)
    def _():
        for s in range(spb):
            prev_ref[s] = jnp.zeros(prev_ref.shape[1:], BF16)
            prev_ref[s, q - CONV_PAD_ROWS:q, :] = conv0_ref[s].astype(BF16)
            for k in range(SSD_HEADS // 2):
                g, kk = divmod(k, pairs_per_group)
                st_ref[s, g, :, kk * LANES:(kk + 1) * LANES] = ssm0_ref[s, k].T

    for s in range(spb):
        _ssd_chunk(xz_ref.at[s, :, :XBC_DIM], xz_ref.at[s, :, XBC_DIM:], dt_ref.at[s], cw_ref, cb_ref, dtb_ref, alog_ref, drep_ref, ng_ref,
                   rep_ref, shift_ref, tri_ref, ya_ref.at[s], prev_ref.at[s], st_ref.at[s], q)

    @pl.when(c == nc - 1)
    def _():
        for s in range(spb):
            tail_ref[s] = xz_ref[s, q - CONV_PAD_ROWS:q, :XBC_DIM].astype(F32)
            convo_ref[s] = tail_ref[s, CONV_PAD_ROWS - (CONV_W - 1):CONV_PAD_ROWS, :]
            for k in range(SSD_HEADS // 2):
                g, kk = divmod(k, pairs_per_group)
                ssmo_ref[s, k] = st_ref[s, g, :, kk * LANES:(kk + 1) * LANES].T


def _ssd(proj, dt_raw, conv0, ssm0_all, lw, nseq, seqlen, layer):
    t = proj.shape[0]
    q = min(SSD_CHUNK, seqlen)
    nc = seqlen // q
    spb = next(n for n in (4, 2, 1) if nseq % n == 0)
    state_block = (spb, SSD_HEADS // 2, 2 * SSD_HEAD_DIM, D_STATE)
    ssm0_pairs = ssm0_all.reshape(ssm0_all.shape[0], nseq, *state_block[1:])
    conv0 = jnp.pad(conv0, ((0, 0), (CONV_PAD_ROWS - (CONV_W - 1), 0), (0, 0)))
    proj3 = proj.reshape(nseq, seqlen, PROJ_COLS)
    const = lambda b, c: (0, 0)
    rowblk = lambda b, c: (b, c, 0)
    ya, conv_out, ssm_out = pl.pallas_call(
        functools.partial(_ssd_body, q=q, nc=nc, spb=spb),
        out_shape=(
            jax.ShapeDtypeStruct((nseq, seqlen, D_INNER), BF16),
            jax.ShapeDtypeStruct((nseq, CONV_W - 1, XBC_DIM), F32),
            jax.ShapeDtypeStruct((nseq,) + state_block[1:], F32),
        ),
        grid=(nseq // spb, nc),
        in_specs=[
            pl.BlockSpec((spb, q, XBC_DIM + D_INNER), rowblk),
            pl.BlockSpec((spb, q, LANES), rowblk),
            pl.BlockSpec((spb, CONV_PAD_ROWS, XBC_DIM), lambda b, c: (b, 0, 0)),
            pl.BlockSpec((None,) + state_block, lambda b, c: (layer, b, 0, 0, 0)),
            pl.BlockSpec((CONV_W, XBC_DIM), const),
            pl.BlockSpec((1, XBC_DIM), const),
            pl.BlockSpec((1, LANES), const),
            pl.BlockSpec((1, LANES), const),
            pl.BlockSpec((1, D_INNER), const),
            pl.BlockSpec((1, D_INNER), const),
            pl.BlockSpec((LANES, D_INNER), const),
            pl.BlockSpec(((CONV_W - 1) * q, 2 * q), const),
            pl.BlockSpec((q, q), const),
        ],
        out_specs=(
            pl.BlockSpec((spb, q, D_INNER), rowblk),
            pl.BlockSpec((spb, CONV_W - 1, XBC_DIM), lambda b, c: (b, 0, 0)),
            pl.BlockSpec(state_block, lambda b, c: (b, 0, 0, 0)),
        ),
        scratch_shapes=[
            pltpu.VMEM((spb, q, XBC_DIM), BF16),
            pltpu.VMEM((spb, CONV_PAD_ROWS, XBC_DIM), F32),
            pltpu.VMEM((spb, SSD_GROUPS, D_STATE, HEADS_PER_GROUP * SSD_HEAD_DIM), F32),
        ],
        compiler_params=_params(("parallel", "arbitrary"), 48),
        name="ssd",
    )(proj3, dt_raw.reshape(nseq, seqlen, LANES), conv0, ssm0_pairs, lw["conv_w"], lw["conv_b"],
      lw["dt_bias"], lw["a_log"], lw["d_rep"], lw["ssd_norm"], lw["head_rep"], _shift_matrices(q),
      jnp.tri(q, dtype=BF16))
    return ya.reshape(t, D_INNER), conv_out, ssm_out.reshape(ssm0_all.shape[1:])


def _gmlp_rows(uv_ref, lg_ref, lb_ref, ws_ref, bst_ref, vn_ref, tc, nchunks):
    uv = jax.nn.gelu(uv_ref[...])
    u = uv[:, :GM_WIDTH].astype(F32)
    v = uv[:, GM_WIDTH:].astype(F32)
    mu = jnp.mean(v, axis=-1, keepdims=True)
    dev = v - mu
    var = jnp.mean(dev * dev, axis=-1, keepdims=True)
    vn = dev * lax.rsqrt(var + EPS) * lg_ref[...] + lb_ref[...]
    if vn_ref is not None:
        vn_ref[...] = vn
    vb = vn.astype(BF16)

    row = lax.broadcasted_iota(jnp.int32, (tc, tc), 0)
    col = lax.broadcasted_iota(jnp.int32, (tc, tc), 1)
    tril = row >= col
    mixed = [[None] * GM_GROUPS for _ in range(nchunks)]
    for g in range(GM_GROUPS):
        w = jnp.where(tril, ws_ref[g, :tc, :tc], 0.0).astype(BF16)
        bias = bst_ref[:tc, g:g + 1]
        cols = slice(g * GM_GROUP_DIM, (g + 1) * GM_GROUP_DIM)
        if tc == LANES:
            rhs = jnp.concatenate([vb[r * tc:(r + 1) * tc, cols] for r in range(nchunks)], axis=1)
            out = _dot(w, rhs) + bias
            for r in range(nchunks):
                mixed[r][g] = out[:, r * GM_GROUP_DIM:(r + 1) * GM_GROUP_DIM]
        else:
            for r in range(nchunks):
                mixed[r][g] = _dot(w, vb[r * tc:(r + 1) * tc, cols]) + bias
    full = jnp.concatenate([jnp.concatenate(m, axis=1) for m in mixed], axis=0)
    return (u * full).astype(BF16)


def _merge_body(ya_ref, uvg_ref, x_ref, lg_ref, lb_ref, ws_ref, bst_ref, wa_ref, wb_ref, wo_ref,
                o_ref, *maybe_vn_ref, tc, nchunks):
    d = o_ref.shape[-1]
    uv_ref = uvg_ref.at[:, :2 * GM_WIDTH]
    ga = uvg_ref[:, 2 * GM_WIDTH:2 * GM_WIDTH + d]
    gb = uvg_ref[:, 2 * GM_WIDTH + d:]
    pa = _dot(ya_ref[...], wa_ref[...])
    yb = _gmlp_rows(uv_ref, lg_ref, lb_ref, ws_ref, bst_ref, maybe_vn_ref[0] if maybe_vn_ref else None, tc, nchunks)
    merged = (jax.nn.sigmoid(ga.astype(F32)) * pa + jax.nn.sigmoid(gb.astype(F32)) * _dot(yb, wb_ref[...]))
    o_ref[...] = x_ref[...] + _dot(merged.astype(BF16), wo_ref[...])


def _resident(block_shape, index_map):
    return pl.BlockSpec(block_shape, index_map, pipeline_mode=pl.Buffered(1))


def _merge(ya, proj, x, lw, wa, wb, wo, layer, seqlen, keep_v):
    t, d = x.shape
    tm = _row_tile(t, 256)
    tc = min(GM_CHUNK, seqlen)
    uvg_cols = PROJ_COLS - UV_OFF
    assert UV_OFF % uvg_cols == 0 and GATE_OFF == UV_OFF + 2 * GM_WIDTH
    whole = lambda i: (layer, 0, 0)
    const = lambda i: (0, 0)
    out_shape = [jax.ShapeDtypeStruct((t, d), F32)]
    out_specs = [pl.BlockSpec((tm, d), lambda i: (i, 0))]
    if keep_v:
        out_shape.append(jax.ShapeDtypeStruct((t, GM_WIDTH), F32))
        out_specs.append(pl.BlockSpec((tm, GM_WIDTH), lambda i: (i, 0)))
    outs = pl.pallas_call(
        functools.partial(_merge_body, tc=tc, nchunks=tm // tc),
        out_shape=tuple(out_shape),
        grid=(t // tm,),
        in_specs=[
            pl.BlockSpec((tm, D_INNER), lambda i: (i, 0)),
            pl.BlockSpec((tm, uvg_cols), lambda i: (i, UV_OFF // uvg_cols)),
            pl.BlockSpec((tm, d), lambda i: (i, 0)),
            pl.BlockSpec((1, GM_WIDTH), const),
            pl.BlockSpec((1, GM_WIDTH), const),
            pl.BlockSpec((GM_GROUPS, GM_CHUNK, GM_CHUNK), lambda i: (0, 0, 0)),
            pl.BlockSpec((GM_CHUNK, GM_GROUPS), const),
            _resident((None, D_INNER, d), whole),
            _resident((None, GM_WIDTH, d), whole),
            _resident((None, d, d), whole),
        ],
        out_specs=tuple(out_specs),
        compiler_params=_params(("parallel",), 56),
        name="merge",
    )(ya, proj, x, lw["gm_ln_g"], lw["gm_ln_b"], lw["gm_ws"], lw["gm_bs_t"], wa, wb, wo)
    return (outs[0], outs[1]) if keep_v else (outs[0], None)


def _ple_body(x_ref, pe_ref, g_ref, wg_ref, wp_ref, *rest, final):
    o_ref = rest[-1]
    half = x_ref.shape[0] // 2
    for r in range(2):
        rows = slice(r * half, (r + 1) * half)
        x = x_ref[rows, :]
        h = _rmsnorm(x, g_ref[...]).astype(BF16)
        out = x + _dot(pe_ref[rows, :].astype(BF16), wp_ref[...]) * jax.nn.sigmoid(_dot(h, wg_ref[...]))
        if final:
            out = _rmsnorm(out, rest[0][...])
        o_ref[rows, :] = out


def _ple(x, pe, g, wg, wp, final_g, layer):
    t, d = x.shape
    tm = _row_tile(t, 512)
    final = final_g is not None
    const = lambda i: (0, 0)
    whole = lambda i: (layer, 0, 0)
    in_specs = [
        pl.BlockSpec((tm, d), lambda i: (i, 0)),
        pl.BlockSpec((None, tm, pe.shape[-1]), lambda i: (layer, i, 0)),
        pl.BlockSpec((1, d), const),
        _resident((None, d, d), whole),
        _resident((None, pe.shape[-1], d), whole),
    ]
    args = [x, pe, g, wg, wp]
    if final:
        in_specs.append(pl.BlockSpec((1, d), const))
        args.append(final_g)
    return pl.pallas_call(
        functools.partial(_ple_body, final=final),
        out_shape=jax.ShapeDtypeStruct((t, d), F32),
        grid=(t // tm,),
        in_specs=in_specs,
        out_specs=pl.BlockSpec((tm, d), lambda i: (i, 0)),
        compiler_params=_params(("parallel",), 48),
        name="ple_final" if final else "ple",
    )(*args)


def _stacked_weights(w):
    stacks = {k: w[k].astype(BF16) for k in (
        "ffn1_w_gate", "ffn1_w_up", "ffn1_w_down", "ffn2_w_gate", "ffn2_w_up", "ffn2_w_down",
        "w_branch_ssd", "w_branch_gmlp", "w_out", "ple_w_gate", "ple_w_proj")}
    w_in_t = jnp.swapaxes(w["w_in"], 1, 2).astype(BF16)
    dt_lo = D_INNER + XBC_DIM
    stacks["w_in_t"] = w_in_t
    stacks["w_dt_t"] = jnp.pad(w_in_t[:, dt_lo:dt_lo + SSD_HEADS, :], ((0, 0), (0, LANES - SSD_HEADS), (0, 0)))
    return stacks


def _layer_weights(i, w):
    pad_heads = lambda v: jnp.pad(v[i].reshape(1, SSD_HEADS), ((0, 0), (0, LANES - SSD_HEADS)))
    rowvec = lambda v: v[i].reshape(1, -1)
    head_of_channel = jnp.arange(D_INNER, dtype=jnp.int32) // SSD_HEAD_DIM
    head_rep = (jnp.arange(LANES, dtype=jnp.int32)[:, None] == head_of_channel[None, :]).astype(BF16)
    return {
        "ffn1_norm": rowvec(w["ffn1_norm"]),
        "mix_norm": rowvec(w["mix_norm"]),
        "conv_w": w["conv_w"][i],
        "conv_b": rowvec(w["conv_b"]),
        "dt_bias": pad_heads(w["dt_bias"]),
        "a_log": pad_heads(w["a_log"]),
        "d_rep": jnp.repeat(w["d_skip"][i], SSD_HEAD_DIM).reshape(1, D_INNER),
        "ssd_norm": rowvec(w["ssd_norm"]),
        "head_rep": head_rep,
        "gm_ln_g": rowvec(w["gm_ln_g"]),
        "gm_ln_b": rowvec(w["gm_ln_b"]),
        "gm_ws": w["gm_ws"][i],
        "gm_bs_t": w["gm_bs"][i].T,
        "ffn2_norm": rowvec(w["ffn2_norm"]),
        "ple_norm": rowvec(w["ple_norm"]),
    }


def _trunk(x, pe, conv_state, ssm_state, sw, layers, final_norm, keep_v):
    nseq, seqlen, d = x.shape
    depth = len(layers)
    t = nseq * seqlen
    x = x.reshape(t, d)
    new_conv, new_ssm, new_v = [], [], []
    for i, lw in enumerate(layers):
        x = _ffn(x, lw["ffn1_norm"], sw["ffn1_w_gate"], sw["ffn1_w_up"], sw["ffn1_w_down"], i)
        proj, dt_raw = _inproj(x, lw["mix_norm"], sw["w_in_t"], sw["w_dt_t"], i)
        ya, conv_out, ssm_out = _ssd(proj, dt_raw, conv_state[i], ssm_state, lw, nseq, seqlen, i)
        x, vn = _merge(ya, proj, x, lw, sw["w_branch_ssd"], sw["w_branch_gmlp"], sw["w_out"], i, seqlen, keep_v)
        x = _ffn(x, lw["ffn2_norm"], sw["ffn2_w_gate"], sw["ffn2_w_up"], sw["ffn2_w_down"], i)
        x = _ple(x, pe.reshape(depth, t, -1), lw["ple_norm"], sw["ple_w_gate"], sw["ple_w_proj"],
                 final_norm if i == depth - 1 else None, i)
        new_conv.append(conv_out)
        new_ssm.append(ssm_out)
        if keep_v:
            new_v.append(vn.reshape(nseq, seqlen, GM_WIDTH))
    y = x.reshape(nseq, seqlen, d)
    return y, jnp.stack(new_conv), jnp.stack(new_ssm), (jnp.stack(new_v) if keep_v else None)


def kernel(x_prompt, x_sample, p_prompt, p_sample, state_conv, state_ssm, ffn1_norm, ffn1_w_gate, ffn1_w_up, ffn1_w_down, mix_norm, w_in, conv_w, conv_b, dt_bias, a_log, d_skip, ssd_norm, gm_ln_g, gm_ln_b, gm_ws, gm_bs, w_branch_ssd, w_branch_gmlp, w_out, ffn2_norm, ffn2_w_gate, ffn2_w_up, ffn2_w_down, ple_norm, ple_w_gate, ple_w_proj, final_norm):
    w = {
        "ffn1_norm": ffn1_norm, "ffn1_w_gate": ffn1_w_gate, "ffn1_w_up": ffn1_w_up, "ffn1_w_down": ffn1_w_down,
        "mix_norm": mix_norm, "w_in": w_in, "conv_w": conv_w, "conv_b": conv_b,
        "dt_bias": dt_bias, "a_log": a_log, "d_skip": d_skip, "ssd_norm": ssd_norm,
        "gm_ln_g": gm_ln_g, "gm_ln_b": gm_ln_b, "gm_ws": gm_ws, "gm_bs": gm_bs,
        "w_branch_ssd": w_branch_ssd, "w_branch_gmlp": w_branch_gmlp, "w_out": w_out,
        "ffn2_norm": ffn2_norm, "ffn2_w_gate": ffn2_w_gate, "ffn2_w_up": ffn2_w_up, "ffn2_w_down": ffn2_w_down,
        "ple_norm": ple_norm, "ple_w_gate": ple_w_gate, "ple_w_proj": ple_w_proj,
    }
    depth = w_in.shape[0]
    layers = [_layer_weights(i, w) for i in range(depth)]
    sw = _stacked_weights(w)
    fnorm = final_norm.reshape(1, -1)
    bp = x_prompt.shape[0]
    conv0 = jnp.zeros((depth, bp, CONV_W - 1, XBC_DIM), F32)
    ssm0 = jnp.zeros((depth, bp, SSD_HEADS, SSD_HEAD_DIM, D_STATE), F32)
    y_prompt, conv_prompt, ssm_prompt, _ = _trunk(x_prompt, p_prompt, conv0, ssm0, sw, layers, fnorm, False)
    y_sample, conv_sample, ssm_sample, v_sample = _trunk(x_sample, p_sample, state_conv, state_ssm, sw, layers, fnorm, True)
    return (y_prompt, y_sample, ssm_prompt, conv_prompt, ssm_sample, conv_sample, v_sample)
```
